```python
import jax, jax.numpy as jnp
from jax import lax
import numpy as np

D_MODEL = 1024
BATCH = 8
SEQ = 4096
DEPTH = 2

GRID_W = 64
CTX_LEN = 256
EPS = 1e-6
LB_FLOOR = 1e-30
MASK_SCORE = -1e4
HGRN_WIDTH = D_MODEL // 2
HGRN_HEAD_DIM = 128
HGRN_HEADS = HGRN_WIDTH // HGRN_HEAD_DIM
POOL_WIDTH = D_MODEL - HGRN_WIDTH
POOL_WINDOWS = (2, 4, 8, 16)
POOL_GROUPS = len(POOL_WINDOWS)
POOL_GROUP_DIM = POOL_WIDTH // POOL_GROUPS
IN_COLS = 5 * HGRN_WIDTH + POOL_WIDTH
CHUNK = 64
N_EXPERTS = 64
TOP_K = 6
N_EXPERT_GROUPS = 8
TOPK_GROUPS = 4
EXPERT_HIDDEN = 256
SHARED_HIDDEN = 256
ROUTED_SCALE = 2.5
MOE_BLOCK = 128

kernel_name = "hgrn2_pool_moe_prefix_dit"


def rmsnorm(x, w):
    xf = x.astype(jnp.float32)
    y = xf * lax.rsqrt(jnp.mean(xf * xf, axis=-1, keepdims=True) + EPS)
    return (y * w.astype(jnp.float32)).astype(x.dtype)


def split_heads(t):
    return t.reshape(t.shape[:-1] + (HGRN_HEADS, HGRN_HEAD_DIM))


def forget_gates(f_logit, lb):
    z = split_heads(f_logit).astype(jnp.float32)
    k = (1.0 - lb) * jax.nn.sigmoid(-z)
    log_f = jnp.logaddexp(jnp.log(jnp.maximum(lb, LB_FLOOR)), jnp.log1p(-lb) + jax.nn.log_sigmoid(z))
    return k, log_f


def gla_chunk_scan(q, k, v, g, s0, with_output):
    bsz, seq_len, n_heads, dv = v.shape
    n_chunks = seq_len // CHUNK

    def to_chunks(t):
        t = t.astype(jnp.float32).reshape(bsz, n_chunks, CHUNK, n_heads, t.shape[-1])
        return jnp.moveaxis(t, 1, 0)

    incl = jnp.tril(jnp.ones((CHUNK, CHUNK), dtype=bool))[None, :, :, None, None]

    def step(state, chunk):
        if with_output:
            k_c, v_c, g_c, q_c = chunk
        else:
            k_c, v_c, g_c = chunk
        b_cum = jnp.cumsum(g_c, axis=1)
        b_tot = b_cum[:, -1]
        k_end = k_c * jnp.exp(b_tot[:, None] - b_cum)
        new_state = jnp.exp(b_tot)[..., None] * state + jnp.einsum("bchk,bchv->bhkv", k_end, v_c)
        if not with_output:
            return new_state, None
        o_inter = jnp.einsum("bchk,bhkv->bchv", q_c * jnp.exp(b_cum), state)
        rel = b_cum[:, :, None] - b_cum[:, None]
        decay = jnp.where(incl, jnp.exp(jnp.minimum(rel, 0.0)), 0.0)
        scores = jnp.einsum("bthk,bshk,btshk->bhts", q_c, k_c, decay)
        o_intra = jnp.einsum("bhts,bshv->bthv", scores, v_c)
        return new_state, o_inter + o_intra

    xs = (to_chunks(k), to_chunks(v), to_chunks(g)) + ((to_chunks(q),) if with_output else ())
    final_state, o = lax.scan(step, s0, xs)
    if with_output:
        o = jnp.moveaxis(o, 0, 1).reshape(bsz, seq_len, n_heads, dv)
    return o, final_state


def gated_head_norm(o, gate_logits, w, dtype):
    y = o * lax.rsqrt(jnp.mean(o * o, axis=-1, keepdims=True) + EPS) * w.astype(jnp.float32)
    y = y * jax.nn.silu(split_heads(gate_logits).astype(jnp.float32))
    return y.reshape(y.shape[:-2] + (HGRN_WIDTH,)).astype(dtype)


def hgrn2_mixer(px, pc, lb_layer, gnorm_w, ctx_out):
    W = HGRN_WIDTH

    def q_of(p):
        return jax.nn.silu(split_heads(p[..., 3 * W:4 * W]).astype(jnp.float32)) * HGRN_HEAD_DIM ** -0.5

    def v_of(p):
        return split_heads(p[..., 2 * W:3 * W]).astype(jnp.float32)

    q_x, v_x, v_c = q_of(px), v_of(px), v_of(pc)
    q_c = q_of(pc) if ctx_out else None
    s0 = jnp.zeros((px.shape[0], HGRN_HEADS, HGRN_HEAD_DIM, HGRN_HEAD_DIM), jnp.float32)
    o_x_dirs, o_c_dirs = [], []
    for d in range(2):
        flip = (lambda t: jnp.flip(t, axis=1)) if d == 1 else (lambda t: t)
        lb = lb_layer[d].reshape(HGRN_HEADS, HGRN_HEAD_DIM)
        k_x, g_x = forget_gates(px[..., d * W:(d + 1) * W], lb)
        k_c, g_c = forget_gates(pc[..., d * W:(d + 1) * W], lb)
        o_c, s_c = gla_chunk_scan(flip(q_c) if ctx_out else None, flip(k_c), flip(v_c), flip(g_c), s0, ctx_out)
        o_x, _ = gla_chunk_scan(flip(q_x), flip(k_x), flip(v_x), flip(g_x), s_c, True)
        o_x_dirs.append(flip(o_x))
        if ctx_out:
            o_c_dirs.append(flip(o_c))
    y_x = gated_head_norm(o_x_dirs[0] + o_x_dirs[1], px[..., 4 * W:5 * W], gnorm_w, px.dtype)
    y_c = gated_head_norm(o_c_dirs[0] + o_c_dirs[1], pc[..., 4 * W:5 * W], gnorm_w, pc.dtype) if ctx_out else None
    return y_x, y_c


def multiscale_pool(x):
    n = x.shape[-2]
    xf = x.astype(jnp.float32)
    prefix = jnp.concatenate([jnp.zeros_like(xf[..., :1, :]), jnp.cumsum(xf, axis=-2)], axis=-2)
    pos = jnp.arange(n)
    outs = []
    for gi, w in enumerate(POOL_WINDOWS):
        lo = jnp.clip(pos - w // 2, 0, n)
        hi = jnp.clip(pos + w - w // 2, 0, n)
        sl = slice(gi * POOL_GROUP_DIM, (gi + 1) * POOL_GROUP_DIM)
        pg = prefix[..., sl]
        window_sum = jnp.take(pg, hi, axis=-2) - jnp.take(pg, lo, axis=-2)
        mean = window_sum / (hi - lo).astype(jnp.float32)[:, None]
        outs.append(mean - xf[..., sl])
    return jnp.concatenate(outs, axis=-1).astype(x.dtype)


def pool_mixer(p, w_pool_l, scale_l, rows):
    bsz, n, ch = p.shape
    if rows is None:
        pooled = multiscale_pool(p)
    else:
        pooled = multiscale_pool(p.reshape(bsz, rows, GRID_W, ch)).reshape(bsz, n, ch)
    grouped = pooled.reshape(bsz, n, POOL_GROUPS, POOL_GROUP_DIM)
    mixed = jnp.einsum("bngc,gcd->bngd", grouped, w_pool_l)
    return mixed.reshape(bsz, n, ch) * scale_l


def moe(h, w_router, r_bias, w1, w3, w2, sw1, sw3, sw2):
    n_tok, d = h.shape
    scores = jax.nn.sigmoid((h @ w_router).astype(jnp.float32))
    biased = scores + r_bias.astype(jnp.float32)
    per_group = N_EXPERTS // N_EXPERT_GROUPS
    grp_score = lax.top_k(biased.reshape(n_tok, N_EXPERT_GROUPS, per_group), 2)[0].sum(-1)
    _, top_grp = lax.top_k(grp_score, TOPK_GROUPS)
    keep = jnp.any(top_grp[:, :, None] == jnp.arange(N_EXPERT_GROUPS)[None, None, :], axis=1)
    masked = jnp.where(jnp.repeat(keep, per_group, axis=1), biased, MASK_SCORE)
    _, idx = lax.top_k(masked, TOP_K)
    gate = jnp.take_along_axis(scores, idx, axis=1)
    gate = gate / jnp.sum(gate, axis=-1, keepdims=True) * ROUTED_SCALE
    n_assign = n_tok * TOP_K
    n_blocks = -(-n_assign // MOE_BLOCK) + N_EXPERTS
    flat_e = idx.reshape(n_assign)
    flat_t = jnp.arange(n_assign, dtype=jnp.int32) // TOP_K
    order = jnp.argsort(flat_e)
    se, st, sw = flat_e[order], flat_t[order], gate.reshape(n_assign)[order]
    counts = jnp.bincount(flat_e, length=N_EXPERTS)
    padded = (counts + MOE_BLOCK - 1) // MOE_BLOCK * MOE_BLOCK
    pad_end = jnp.cumsum(padded)
    pad_start = pad_end - padded
    start = jnp.cumsum(counts) - counts
    dest = pad_start[se] + jnp.arange(n_assign, dtype=jnp.int32) - start[se]
    rows_total = n_blocks * MOE_BLOCK
    tok_buf = jnp.full((rows_total,), n_tok, jnp.int32).at[dest].set(st)
    w_buf = jnp.zeros((rows_total,), jnp.float32).at[dest].set(sw)
    blk_e = jnp.minimum(jnp.searchsorted(pad_end, jnp.arange(n_blocks, dtype=jnp.int32) * MOE_BLOCK, side="right"), N_EXPERTS - 1)
    h_pad = jnp.concatenate([h, jnp.zeros((1, d), h.dtype)], axis=0)

    def expert_block(acc, blk):
        tok, wt, e = blk
        xb = h_pad[tok]
        y = (jax.nn.silu(xb @ w1[e]) * (xb @ w3[e])) @ w2[e]
        return acc.at[tok].add(y * wt[:, None].astype(y.dtype)), None

    routed, _ = lax.scan(expert_block, jnp.zeros((n_tok + 1, d), h.dtype),
                         (tok_buf.reshape(n_blocks, MOE_BLOCK), w_buf.reshape(n_blocks, MOE_BLOCK), blk_e))
    shared = (jax.nn.silu(h @ sw1) * (h @ sw3)) @ sw2
    return routed[:n_tok] + shared


def setup_inputs(seed: int = 0) -> dict:
    key = jax.random.key(seed)
    ks = jax.random.split(key, 24)
    D, E, F, FS = D_MODEL, N_EXPERTS, EXPERT_HIDDEN, SHARED_HIDDEN

    def nrm(k, shape, scale):
        return jax.random.normal(k, shape, jnp.float32) * scale

    return {
        "x": nrm(ks[0], (BATCH, SEQ, D), 1.0),
        "c": nrm(ks[1], (BATCH, D), 1.0),
        "ctx": nrm(ks[2], (BATCH, CTX_LEN, D), 1.0),
        "c_ctx": nrm(ks[3], (D,), 1.0),
        "norm1_w": 1.0 + nrm(ks[4], (DEPTH, D), 0.01),
        "norm2_w": 1.0 + nrm(ks[5], (DEPTH, D), 0.01),
        "w_ada": nrm(ks[6], (DEPTH, D, 6 * D), 0.5 * D ** -0.5),
        "b_ada": nrm(ks[7], (DEPTH, 6 * D), 0.02),
        "w_in": nrm(ks[8], (DEPTH, D, IN_COLS), D ** -0.5),
        "hgrn_lb_logits": nrm(ks[9], (DEPTH, 2, HGRN_WIDTH), 0.5),
        "hgrn_gnorm_w": 1.0 + nrm(ks[10], (DEPTH, HGRN_HEAD_DIM), 0.01),
        "w_pool": nrm(ks[11], (DEPTH, POOL_GROUPS, POOL_GROUP_DIM, POOL_GROUP_DIM), POOL_GROUP_DIM ** -0.5),
        "pool_scale": 1.0 + nrm(ks[12], (DEPTH, POOL_WIDTH), 0.01),
        "w_out": nrm(ks[13], (DEPTH, D, D), D ** -0.5),
        "w_router": nrm(ks[14], (DEPTH, D, E), D ** -0.5),
        "router_bias": nrm(ks[15], (DEPTH, E), 0.01),
        "w1_experts": nrm(ks[16], (DEPTH, E, D, F), D ** -0.5),
        "w3_experts": nrm(ks[17], (DEPTH, E, D, F), D ** -0.5),
        "w2_experts": nrm(ks[18], (DEPTH, E, F, D), F ** -0.5),
        "w1_shared": nrm(ks[19], (DEPTH, D, FS), D ** -0.5),
        "w3_shared": nrm(ks[20], (DEPTH, D, FS), D ** -0.5),
        "w2_shared": nrm(ks[21], (DEPTH, FS, D), FS ** -0.5),
        "final_norm_w": 1.0 + nrm(ks[22], (D,), 0.01),
    }


def reference(x, c, ctx, c_ctx, norm1_w, norm2_w, w_ada, b_ada, w_in, hgrn_lb_logits, hgrn_gnorm_w,
              w_pool, pool_scale, w_out, w_router, router_bias, w1_experts, w3_experts, w2_experts,
              w1_shared, w3_shared, w2_shared, final_norm_w):
    bsz, seq_len, d = x.shape
    ctx_len = ctx.shape[1]
    rows = seq_len // GRID_W
    W = HGRN_WIDTH
    p_lb = jax.nn.softmax(hgrn_lb_logits.astype(jnp.float32), axis=0)
    lb_all = jnp.cumsum(p_lb, axis=0) - p_lb[0:1]
    for l in range(DEPTH):
        last = l == DEPTH - 1
        mod_x = (jax.nn.silu(c) @ w_ada[l] + b_ada[l])[:, None, :]
        mod_c = (jax.nn.silu(c_ctx) @ w_ada[l] + b_ada[l])[None, None, :]
        sh1, sc1, gt1, sh2, sc2, gt2 = jnp.split(mod_x, 6, axis=-1)
        csh1, csc1, cgt1, csh2, csc2, cgt2 = jnp.split(mod_c, 6, axis=-1)
        hx = rmsnorm(x, norm1_w[l]) * (1 + sc1) + sh1
        hc = rmsnorm(ctx, norm1_w[l]) * (1 + csc1) + csh1
        px = hx @ w_in[l]
        pc = hc @ (w_in[l][:, :3 * W] if last else w_in[l])
        hg_x, hg_c = hgrn2_mixer(px, pc, lb_all[l], hgrn_gnorm_w[l], not last)
        pool_x = pool_mixer(px[..., 5 * W:], w_pool[l], pool_scale[l], rows)
        x = x + gt1 * (jnp.concatenate([hg_x, pool_x], axis=-1) @ w_out[l])
        hx2 = rmsnorm(x, norm2_w[l]) * (1 + sc2) + sh2
        if last:
            y = moe(hx2.reshape(-1, d), w_router[l], router_bias[l], w1_experts[l], w3_experts[l], w2_experts[l],
                    w1_shared[l], w3_shared[l], w2_shared[l])
            x = x + gt2 * y.reshape(bsz, seq_len, d)
        else:
            pool_c = pool_mixer(pc[..., 5 * W:], w_pool[l], pool_scale[l], None)
            ctx = ctx + cgt1 * (jnp.concatenate([hg_c, pool_c], axis=-1) @ w_out[l])
            hc2 = rmsnorm(ctx, norm2_w[l]) * (1 + csc2) + csh2
            tokens = jnp.concatenate([hc2.reshape(-1, d), hx2.reshape(-1, d)], axis=0)
            y = moe(tokens, w_router[l], router_bias[l], w1_experts[l], w3_experts[l], w2_experts[l],
                    w1_shared[l], w3_shared[l], w2_shared[l])
            ctx = ctx + cgt2 * y[:bsz * ctx_len].reshape(bsz, ctx_len, d)
            x = x + gt2 * y[bsz * ctx_len:].reshape(bsz, seq_len, d)
    return rmsnorm(x, final_norm_w)
```

```python
import functools

import numpy as np
import jax
import jax.numpy as jnp
from jax import lax
from jax.experimental import pallas as pl
from jax.experimental.pallas import tpu as pltpu

F32 = jnp.float32
BF16 = jnp.bfloat16

EPS = 1e-6
LB_FLOOR = 1e-30
MASK_SCORE = -1e4
GRID_W = 64
HEAD_DIM = 128
N_HEADS = 4
HGRN_W = N_HEADS * HEAD_DIM
POOL_WINDOWS = (2, 4, 8, 16)
POOL_GROUPS = len(POOL_WINDOWS)
POOL_GROUP_DIM = 128
POOL_W = POOL_GROUPS * POOL_GROUP_DIM
N_EXPERTS = 64
TOP_K = 6
N_EXPERT_GROUPS = 8
EXPERTS_PER_GROUP = N_EXPERTS // N_EXPERT_GROUPS
TOPK_GROUPS = 4
ROUTED_SCALE = 2.5

TILE = 256
N_LEVELS = 8
MOE_TOKEN_TILE = 1024
VMEM_LIMIT = 56 * 1024 * 1024
NEG_INF = float("-inf")


def _sigmoid(z):
    return 1.0 / (1.0 + jnp.exp(-z))


def _silu(z):
    return z * _sigmoid(z)


def _dot(a, b):
    return jnp.dot(a, b, preferred_element_type=F32)


def _dot_nt(a, b):
    return lax.dot_general(a, b, (((1,), (1,)), ((), ())), preferred_element_type=F32)


def _dot_tn(a, b):
    return lax.dot_general(a, b, (((0,), (0,)), ((), ())), preferred_element_type=F32)


def _split3(g):
    hi = g.astype(BF16)
    r1 = g - hi.astype(F32)
    mid = r1.astype(BF16)
    lo = (r1 - mid.astype(F32)).astype(BF16)
    return hi, mid, lo


def _rmsnorm(x, w):
    return x * lax.rsqrt(jnp.mean(x * x, axis=-1, keepdims=True) + EPS) * w


def _level_matrices():
    t = np.arange(TILE)
    xor = t[:, None] ^ t[None, :]
    lvl = np.floor(np.log2(np.maximum(xor, 1))).astype(np.int32)
    lf = np.where(t[:, None] > t[None, :], lvl, -1).astype(np.int32)
    return lf, lf.T.copy()


def _tri_matrices():
    t = np.arange(TILE)
    lower = (t[:, None] >= t[None, :]).astype(np.float32)
    return lower, lower.T.copy()


def _pool_matrices():
    a = np.zeros((2, POOL_GROUPS, TILE, TILE), np.float32)
    inv = np.zeros((2, TILE, POOL_W), np.float32)
    t = np.arange(TILE)
    for kind, n in ((0, TILE), (1, GRID_W)):
        pos, base = t % n, (t // n) * n
        for gi, w in enumerate(POOL_WINDOWS):
            lo = np.clip(pos - w // 2, 0, n)
            hi = np.clip(pos + w - w // 2, 0, n)
            s = t[None, :]
            a[kind, gi] = ((s >= (base + lo)[:, None]) & (s < (base + hi)[:, None])).astype(np.float32)
            inv[kind, :, gi * POOL_GROUP_DIM:(gi + 1) * POOL_GROUP_DIM] = (1.0 / (hi - lo))[:, None]
    return a, inv


def _ada_kernel(c_ref, w_ref, b_ref, o_ref):
    o_ref[0] = jnp.dot(_silu(c_ref[...]), w_ref[0], preferred_element_type=F32,
                       precision=lax.Precision.HIGHEST) + b_ref[0]


def _ada_table(cvec, w_ada, b_ada):
    depth, d, n = w_ada.shape
    nb = 4
    rows = cvec.shape[0]
    return pl.pallas_call(
        _ada_kernel,
        out_shape=jax.ShapeDtypeStruct((depth, rows, n), F32),
        grid=(depth, nb),
        in_specs=[pl.BlockSpec((rows, d), lambda l, j: (0, 0)),
                  pl.BlockSpec((1, d, n // nb), lambda l, j: (l, 0, j)),
                  pl.BlockSpec((1, 1, n // nb), lambda l, j: (l, 0, j))],
        out_specs=pl.BlockSpec((1, rows, n // nb), lambda l, j: (l, 0, j)),
        compiler_params=pltpu.CompilerParams(dimension_semantics=("arbitrary", "arbitrary"),
                                             vmem_limit_bytes=VMEM_LIMIT),
        name="ada_table",
    )(cvec, w_ada, b_ada.reshape(depth, 1, n))


def _reference_rows(b, lvl, backward):
    w = 1 << lvl
    pm = w if backward else w - 1
    if w >= 4:
        n = TILE // (2 * w)
        row = b.reshape(n, 2 * w, HEAD_DIM)[:, pm:pm + 1, :]
        return jnp.broadcast_to(row, (n, 2 * w, HEAD_DIM)).reshape(TILE, HEAD_DIM)
    pos = lax.broadcasted_iota(jnp.int32, (TILE, HEAD_DIM), 0) & (2 * w - 1)
    out = b
    for p in range(2 * w):
        if p == pm:
            continue
        out = jnp.where(pos == p, pltpu.roll(b, (p - pm) % TILE, axis=0), out)
    return out


def _mixer_in_kernel(has_prev, *refs):
    if has_prev:
        xb_ref, yr_ref, modp_ref = refs[:3]
        refs = refs[3:]
    else:
        x_ref = refs[0]
        refs = refs[1:]
    (mod_ref, nw_ref, win_ref, lb_ref, tri_ref, lvl_ref, pa_ref, pinv_ref, wp_ref, ps_ref) = refs[:10]
    outs = refs[10:]
    if has_prev:
        oi_ref, qe_ref, ut_ref, a_ref, og_ref, po_ref, xo_ref = outs
        x = xb_ref[0] + modp_ref[0, 0][5:6] * yr_ref[0]
        xo_ref[0] = x
    else:
        oi_ref, qe_ref, ut_ref, a_ref, og_ref, po_ref = outs
        x = x_ref[0]

    m = mod_ref[0, 0]
    hx = (_rmsnorm(x, nw_ref[...]) * (1.0 + m[1:2]) + m[0:1]).astype(BF16)

    def proj(col):
        return _dot(hx, win_ref[:, col * HGRN_W:(col + 1) * HGRN_W])

    v = proj(2).astype(BF16)
    q = _silu(proj(3)) * (HEAD_DIM ** -0.5)
    og_ref[0] = _silu(proj(4)).astype(BF16)

    p = proj(5)
    pb = p.astype(BF16)
    pinv = pinv_ref[0]
    for g in range(POOL_GROUPS):
        sl = slice(g * POOL_GROUP_DIM, (g + 1) * POOL_GROUP_DIM)
        pooled = _dot(pa_ref[0, g], pb[:, sl]) * pinv[:, sl] - p[:, sl]
        po_ref[0, :, sl] = (_dot(pooled.astype(BF16), wp_ref[g]) * ps_ref[:, sl]).astype(BF16)

    ks, bs = [], []
    for d in range(2):
        z = proj(d)
        lb = lb_ref[d:d + 1, :]
        kk = (1.0 - lb) * _sigmoid(-z)
        g = jnp.log(jnp.maximum(lb, LB_FLOOR) + (1.0 - lb) * _sigmoid(z))
        hi, mid, lo = _split3(g)
        tri = tri_ref[d]
        b = _dot(tri, hi) + _dot(tri, mid) + _dot(tri, lo)
        b_tot = b[TILE - 1:TILE, :] if d == 0 else b[0:1, :]
        qe_ref[0, :, d * HGRN_W:(d + 1) * HGRN_W] = (q * jnp.exp(b)).astype(BF16)
        k_end = (kk * jnp.exp(b_tot - b)).astype(BF16)
        a_tot = jnp.exp(b_tot)
        for h in range(N_HEADS):
            sl = slice(h * HEAD_DIM, (h + 1) * HEAD_DIM)
            ut_ref[0, 0, d, h] = _dot_tn(v[:, sl], k_end[:, sl])
            a_ref[0, 0, d, h] = a_tot[:, sl]
        ks.append(kk)
        bs.append(b)

    rows = lax.broadcasted_iota(jnp.int32, (TILE, HEAD_DIM), 0)
    eye = (lax.broadcasted_iota(jnp.int32, (TILE, TILE), 0)
           == lax.broadcasted_iota(jnp.int32, (TILE, TILE), 1))
    for h in range(N_HEADS):
        sl = slice(h * HEAD_DIM, (h + 1) * HEAD_DIM)
        qh = q[:, sl]
        diag = jnp.sum(qh * (ks[0][:, sl] + ks[1][:, sl]), axis=-1, keepdims=True)
        scores = jnp.where(eye, diag, 0.0)
        for d in range(2):
            kh, bh = ks[d][:, sl], bs[d][:, sl]
            lmat = lvl_ref[d]
            for lvl in range(N_LEVELS):
                upper = ((rows >> lvl) & 1) == 1
                is_t = upper if d == 0 else jnp.logical_not(upper)
                ref_b = _reference_rows(bh, lvl, d == 1)
                arg = jnp.where(is_t, bh - ref_b, ref_b - bh)
                xm = (jnp.where(is_t, qh, kh) * jnp.exp(arg)).astype(BF16)
                scores = jnp.where(lmat == lvl, _dot_nt(xm, xm), scores)
        oi_ref[0, :, sl] = _dot(scores.astype(BF16), v[:, sl])


def _mixer_in(has_prev, xs, modx, nw, win, lb, consts, wp, ps):
    tri, lvlm, pa, pinv = consts
    bsz, ntok, d = xs[0].shape
    nt = ntok // TILE
    tok = lambda w: pl.BlockSpec((1, TILE, w), lambda b, j: (b, j, 0))
    modspec = pl.BlockSpec((1, 1, 6, d), lambda b, j: (b, jnp.minimum(j, 1), 0, 0))
    full = lambda a: pl.BlockSpec(a.shape, lambda b, j: (0,) * a.ndim)
    in_specs = ([tok(d), tok(d), modspec] if has_prev else [tok(d)]) + [
        modspec, full(nw), full(win), full(lb), full(tri), full(lvlm),
        pl.BlockSpec((1,) + pa.shape[1:], lambda b, j: (jnp.minimum(j, 1), 0, 0, 0)),
        pl.BlockSpec((1,) + pinv.shape[1:], lambda b, j: (jnp.minimum(j, 1), 0, 0)),
        full(wp), full(ps)]
    out_shape = [jax.ShapeDtypeStruct((bsz, ntok, HGRN_W), F32),
                 jax.ShapeDtypeStruct((bsz, ntok, 2 * HGRN_W), BF16),
                 jax.ShapeDtypeStruct((bsz, nt, 2, N_HEADS, HEAD_DIM, HEAD_DIM), F32),
                 jax.ShapeDtypeStruct((bsz, nt, 2, N_HEADS, 1, HEAD_DIM), F32),
                 jax.ShapeDtypeStruct((bsz, ntok, HGRN_W), BF16),
                 jax.ShapeDtypeStruct((bsz, ntok, POOL_W), BF16)]
    out_specs = [tok(HGRN_W), tok(2 * HGRN_W),
                 pl.BlockSpec((1, 1, 2, N_HEADS, HEAD_DIM, HEAD_DIM), lambda b, j: (b, j, 0, 0, 0, 0)),
                 pl.BlockSpec((1, 1, 2, N_HEADS, 1, HEAD_DIM), lambda b, j: (b, j, 0, 0, 0, 0)),
                 tok(HGRN_W), tok(POOL_W)]
    if has_prev:
        out_shape.append(jax.ShapeDtypeStruct((bsz, ntok, d), F32))
        out_specs.append(tok(d))
    args = list(xs) + [modx, nw, win, lb, tri, lvlm, pa, pinv, wp, ps]
    return pl.pallas_call(
        functools.partial(_mixer_in_kernel, has_prev),
        out_shape=out_shape, grid=(bsz, nt), in_specs=in_specs, out_specs=out_specs,
        compiler_params=pltpu.CompilerParams(dimension_semantics=("parallel", "parallel"),
                                             vmem_limit_bytes=VMEM_LIMIT),
        name="mixer_in",
    )(*args)


def _scan_kernel(utf_ref, af_ref, utb_ref, ab_ref, sf_ref, sb_ref, accf, accb):
    @pl.when(pl.program_id(1) == 0)
    def _():
        accf[...] = jnp.zeros_like(accf)
        accb[...] = jnp.zeros_like(accb)

    sf_ref[0, 0] = accf[...]
    sb_ref[0, 0] = accb[...]
    accf[...] = af_ref[0, 0, 0] * accf[...] + utf_ref[0, 0, 0]
    accb[...] = ab_ref[0, 0, 0] * accb[...] + utb_ref[0, 0, 0]


def _scan_state(ut, a):
    bsz, nt = ut.shape[:2]
    bwd = lambda s: jnp.where(s == 0, 0, nt - s)
    st = (N_HEADS, HEAD_DIM, HEAD_DIM)
    ut_spec = lambda d, o: pl.BlockSpec((1, 1, 1) + st, lambda b, s: (b, o(s), d, 0, 0, 0))
    a_spec = lambda d, o: pl.BlockSpec((1, 1, 1, N_HEADS, 1, HEAD_DIM), lambda b, s: (b, o(s), d, 0, 0, 0))
    ident = lambda s: s
    return pl.pallas_call(
        _scan_kernel,
        out_shape=[jax.ShapeDtypeStruct((bsz, nt) + st, F32)] * 2,
        grid=(bsz, nt),
        in_specs=[ut_spec(0, ident), a_spec(0, ident), ut_spec(1, bwd), a_spec(1, bwd)],
        out_specs=[pl.BlockSpec((1, 1) + st, lambda b, s: (b, s, 0, 0, 0)),
                   pl.BlockSpec((1, 1) + st, lambda b, s: (b, bwd(s), 0, 0, 0))],
        scratch_shapes=[pltpu.VMEM(st, F32), pltpu.VMEM(st, F32)],
        compiler_params=pltpu.CompilerParams(dimension_semantics=("parallel", "arbitrary"),
                                             vmem_limit_bytes=VMEM_LIMIT),
        name="scan_state",
    )(ut, a, ut, a)


def _route(scores, bias):
    t, e = scores.shape
    lane = lax.broadcasted_iota(jnp.int32, (t, e), 1)
    lane_f = lane.astype(F32)
    grp_f = (lane // EXPERTS_PER_GROUP).astype(F32)
    big = float(4 * e)
    biased = scores + bias

    def first_max(x, idx_f):
        mx = jnp.max(x, axis=1, keepdims=True)
        return mx, jnp.min(jnp.where(x == mx, idx_f, big), axis=1, keepdims=True)

    grp_score = jnp.zeros_like(biased)
    for g in range(N_EXPERT_GROUPS):
        in_g = grp_f == float(g)
        xg = jnp.where(in_g, biased, NEG_INF)
        m1, i1 = first_max(xg, lane_f)
        m2 = jnp.max(jnp.where(lane_f == i1, NEG_INF, xg), axis=1, keepdims=True)
        grp_score = jnp.where(in_g, m1 + m2, grp_score)
    keep = jnp.zeros_like(biased)
    work = grp_score
    for _ in range(TOPK_GROUPS):
        _, gi = first_max(work, grp_f)
        hit = grp_f == gi
        keep = jnp.where(hit, 1.0, keep)
        work = jnp.where(hit, NEG_INF, work)
    work = jnp.where(keep > 0.0, biased, MASK_SCORE)
    sel = jnp.zeros_like(biased)
    for _ in range(TOP_K):
        _, ei = first_max(work, lane_f)
        hit = lane_f == ei
        sel = jnp.where(hit, 1.0, sel)
        work = jnp.where(hit, NEG_INF, work)
    gate = jnp.where(sel > 0.0, scores, 0.0)
    return gate / jnp.sum(gate, axis=1, keepdims=True) * ROUTED_SCALE


def _mixer_out_kernel(x_ref, oi_ref, qe_ref, sf_ref, sb_ref, og_ref, po_ref, mod_ref, gw_ref, wo_ref,
                      nw2_ref, wr_ref, rb_ref, sw1_ref, sw3_ref, sw2_ref,
                      xb_ref, h2_ref, gate_ref):
    m = mod_ref[0, 0]
    parts = []
    for h in range(N_HEADS):
        sl = slice(h * HEAD_DIM, (h + 1) * HEAD_DIM)
        o = (oi_ref[0, :, sl]
             + _dot_nt(qe_ref[0, :, sl], sf_ref[0, 0, h].astype(BF16))
             + _dot_nt(qe_ref[0, :, HGRN_W + h * HEAD_DIM:HGRN_W + (h + 1) * HEAD_DIM],
                       sb_ref[0, 0, h].astype(BF16)))
        y = _rmsnorm(o, gw_ref[...]) * og_ref[0, :, sl].astype(F32)
        parts.append(y.astype(BF16))
    mix = jnp.concatenate(parts + [po_ref[0]], axis=-1)
    x1 = x_ref[0] + m[2:3] * _dot(mix, wo_ref[...])
    h2 = _rmsnorm(x1, nw2_ref[...]) * (1.0 + m[4:5]) + m[3:4]
    h2b = h2.astype(BF16)
    h2_ref[0] = h2b
    shared = _dot((_silu(_dot(h2b, sw1_ref[...])) * _dot(h2b, sw3_ref[...])).astype(BF16), sw2_ref[...])
    xb_ref[0] = x1 + m[5:6] * shared
    logits = jnp.dot(h2, wr_ref[...], preferred_element_type=F32, precision=lax.Precision.HIGHEST)
    gate_ref[0] = _route(_sigmoid(logits), rb_ref[...])


def _mixer_out(x, oi, qe, sf, sb, og, po, modx, gw, wo, nw2, wr, rb, sw1, sw3, sw2):
    bsz, ntok, d = x.shape
    nt = ntok // TILE
    tok = lambda w: pl.BlockSpec((1, TILE, w), lambda b, j: (b, j, 0))
    st = pl.BlockSpec((1, 1, N_HEADS, HEAD_DIM, HEAD_DIM), lambda b, j: (b, j, 0, 0, 0))
    full = lambda a: pl.BlockSpec(a.shape, lambda b, j: (0,) * a.ndim)
    return pl.pallas_call(
        _mixer_out_kernel,
        out_shape=[jax.ShapeDtypeStruct((bsz, ntok, d), F32),
                   jax.ShapeDtypeStruct((bsz, ntok, d), BF16),
                   jax.ShapeDtypeStruct((bsz, ntok, N_EXPERTS), F32)],
        grid=(bsz, nt),
        in_specs=[tok(d), tok(HGRN_W), tok(2 * HGRN_W), st, st, tok(HGRN_W), tok(POOL_W),
                  pl.BlockSpec((1, 1, 6, d), lambda b, j: (b, jnp.minimum(j, 1), 0, 0)),
                  full(gw), full(wo), full(nw2), full(wr), full(rb), full(sw1), full(sw3), full(sw2)],
        out_specs=[tok(d), tok(d), tok(N_EXPERTS)],
        compiler_params=pltpu.CompilerParams(dimension_semantics=("parallel", "parallel"),
                                             vmem_limit_bytes=VMEM_LIMIT),
        name="mixer_out",
    )(x, oi, qe, sf, sb, og, po, modx, gw, wo, nw2, wr, rb, sw1, sw3, sw2)


def _moe_dense_kernel(h_ref, g_ref, w1_ref, w3_ref, w2_ref, y_ref):
    e = pl.program_id(1)
    h = h_ref[...]
    gates = g_ref[...]
    lane = lax.broadcasted_iota(jnp.int32, gates.shape, 1)
    gcol = jnp.sum(jnp.where(lane == e, gates, 0.0), axis=1, keepdims=True)
    hid = (_silu(_dot(h, w1_ref[0])) * _dot(h, w3_ref[0]) * gcol).astype(BF16)
    contrib = _dot(hid, w2_ref[0])

    @pl.when(e == 0)
    def _():
        y_ref[...] = contrib

    @pl.when(e > 0)
    def _():
        y_ref[...] += contrib


def _moe_dense(h, gates, w1, w3, w2):
    ntok, d = h.shape
    ne, _, f = w1.shape
    tt = MOE_TOKEN_TILE
    return pl.pallas_call(
        _moe_dense_kernel,
        out_shape=jax.ShapeDtypeStruct((ntok, d), F32),
        grid=(ntok // tt, ne),
        in_specs=[pl.BlockSpec((tt, d), lambda i, e: (i, 0)),
                  pl.BlockSpec((tt, ne), lambda i, e: (i, 0)),
                  pl.BlockSpec((1, d, f), lambda i, e: (e, 0, 0)),
                  pl.BlockSpec((1, d, f), lambda i, e: (e, 0, 0)),
                  pl.BlockSpec((1, f, d), lambda i, e: (e, 0, 0))],
        out_specs=pl.BlockSpec((tt, d), lambda i, e: (i, 0)),
        compiler_params=pltpu.CompilerParams(dimension_semantics=("parallel", "arbitrary"),
                                             vmem_limit_bytes=VMEM_LIMIT),
        name="moe_dense",
    )(h, gates, w1, w3, w2)


def _final_kernel(xb_ref, yr_ref, mod_ref, nw_ref, o_ref):
    x = xb_ref[0] + mod_ref[0, 0][5:6] * yr_ref[0]
    o_ref[0] = _rmsnorm(x, nw_ref[...])


def _final(xb, yr, modx, nw, n_ctx_tiles):
    bsz, ntok, d = xb.shape
    nt = ntok // TILE - n_ctx_tiles
    lat = pl.BlockSpec((1, TILE, d), lambda b, j: (b, j + n_ctx_tiles, 0))
    return pl.pallas_call(
        _final_kernel,
        out_shape=jax.ShapeDtypeStruct((bsz, nt * TILE, d), F32),
        grid=(bsz, nt),
        in_specs=[lat, lat, pl.BlockSpec((1, 1, 6, d), lambda b, j: (b, 1, 0, 0)),
                  pl.BlockSpec(nw.shape, lambda b, j: (0, 0))],
        out_specs=pl.BlockSpec((1, TILE, d), lambda b, j: (b, j, 0)),
        compiler_params=pltpu.CompilerParams(dimension_semantics=("parallel", "parallel"),
                                             vmem_limit_bytes=VMEM_LIMIT),
        name="final_norm",
    )(xb, yr, modx, nw)


def kernel(x, c, ctx, c_ctx, norm1_w, norm2_w, w_ada, b_ada, w_in, hgrn_lb_logits, hgrn_gnorm_w, w_pool,
           pool_scale, w_out, w_router, router_bias, w1_experts, w3_experts, w2_experts, w1_shared,
           w3_shared, w2_shared, final_norm_w):
    bsz, seq_len, d = x.shape
    ctx_len = ctx.shape[1]
    depth = w_ada.shape[0]
    assert ctx_len == TILE and seq_len % TILE == 0 and seq_len % GRID_W == 0
    ntok = ctx_len + seq_len

    lf, lbm = _level_matrices()
    lower, upper = _tri_matrices()
    pa, pinv = _pool_matrices()
    consts = (jnp.asarray(np.stack([lower, upper]), BF16), jnp.asarray(np.stack([lf, lbm])),
              jnp.asarray(pa, BF16), jnp.asarray(pinv))

    p_lb = jax.nn.softmax(hgrn_lb_logits.astype(F32), axis=0)
    lb_all = jnp.cumsum(p_lb, axis=0) - p_lb[0:1]

    rows = 16
    cvec = jnp.concatenate([c, c_ctx[None, :], jnp.zeros((rows - bsz - 1, d), F32)], axis=0)
    mod = _ada_table(cvec, w_ada, b_ada)
    mod_lat = mod[:, :bsz].reshape(depth, bsz, 6, d)
    mod_ctx = jnp.broadcast_to(mod[:, bsz].reshape(depth, 1, 6, d), (depth, bsz, 6, d))
    modx = jnp.stack([mod_ctx, mod_lat], axis=2)

    xs = (jnp.concatenate([ctx, x], axis=1),)
    for l in range(depth):
        has_prev = l > 0
        outs = _mixer_in(has_prev, xs + ((modx[l - 1],) if has_prev else ()), modx[l], norm1_w[l][None, :],
                         w_in[l].astype(BF16), lb_all[l], consts, w_pool[l].astype(BF16),
                         pool_scale[l][None, :])
        oi, qe, ut, a, og, po = outs[:6]
        x_cur = outs[6] if has_prev else xs[0]
        sf, sb = _scan_state(ut, a)
        xb, h2, gates = _mixer_out(x_cur, oi, qe, sf, sb, og, po, modx[l], hgrn_gnorm_w[l][None, :],
                                   w_out[l].astype(BF16), norm2_w[l][None, :], w_router[l],
                                   router_bias[l][None, :], w1_shared[l].astype(BF16),
                                   w3_shared[l].astype(BF16), w2_shared[l].astype(BF16))
        yr = _moe_dense(h2.reshape(bsz * ntok, d), gates.reshape(bsz * ntok, N_EXPERTS),
                        w1_experts[l].astype(BF16), w3_experts[l].astype(BF16), w2_experts[l].astype(BF16))
        xs = (xb, yr.reshape(bsz, ntok, d))
    return _final(xs[0], xs[1], modx[depth - 1], final_norm_w[None, :], ctx_len // TILE)
```

```python
import functools

import numpy as np
import jax
import jax.numpy as jnp
from jax import lax
from jax.experimental import pallas as pl
from jax.experimental.pallas import tpu as pltpu

F32 = jnp.float32
BF16 = jnp.bfloat16

EPS = 1e-6
LB_FLOOR = 1e-30
MASK_SCORE = -1e4
GRID_W = 64
HEAD_DIM = 128
N_HEADS = 4
HGRN_W = N_HEADS * HEAD_DIM
POOL_WINDOWS = (2, 4, 8, 16)
POOL_GROUPS = len(POOL_WINDOWS)
POOL_GROUP_DIM = 128
POOL_W = POOL_GROUPS * POOL_GROUP_DIM
N_EXPERTS = 64
TOP_K = 6
N_EXPERT_GROUPS = 8
EXPERTS_PER_GROUP = N_EXPERTS // N_EXPERT_GROUPS
TOPK_GROUPS = 4
ROUTED_SCALE = 2.5

TILE = 256
N_LEVELS = 8
GRAN = 16
MAX_GRAN_PER_TILE = (TILE * TOP_K + N_EXPERTS * (GRAN - 1)) // GRAN
GRAN_PER_TILE = -(-MAX_GRAN_PER_TILE * GRAN // TILE) * TILE // GRAN
SORT_ROWS = GRAN_PER_TILE * GRAN
GRAN_PER_BLOCK = 32
VMEM_LIMIT = 56 * 1024 * 1024
NEG_INF = float("-inf")


def _sigmoid(z):
    return 1.0 / (1.0 + jnp.exp(-z))


def _silu(z):
    return z * _sigmoid(z)


def _dot(a, b):
    return jnp.dot(a, b, preferred_element_type=F32)


def _dot_nt(a, b):
    return lax.dot_general(a, b, (((1,), (1,)), ((), ())), preferred_element_type=F32)


def _dot_tn(a, b):
    return lax.dot_general(a, b, (((0,), (0,)), ((), ())), preferred_element_type=F32)


def _split3(g):
    hi = g.astype(BF16)
    r1 = g - hi.astype(F32)
    mid = r1.astype(BF16)
    lo = (r1 - mid.astype(F32)).astype(BF16)
    return hi, mid, lo


def _rmsnorm(x, w):
    return x * lax.rsqrt(jnp.mean(x * x, axis=-1, keepdims=True) + EPS) * w


def _level_matrices():
    t = np.arange(TILE)
    xor = t[:, None] ^ t[None, :]
    lvl = np.floor(np.log2(np.maximum(xor, 1))).astype(np.int32)
    lf = np.where(t[:, None] > t[None, :], lvl, -1).astype(np.int32)
    return lf, lf.T.copy()


def _tri_matrices():
    t = np.arange(TILE)
    lower = (t[:, None] >= t[None, :]).astype(np.float32)
    return lower, lower.T.copy()


def _pool_matrices():
    a = np.zeros((2, POOL_GROUPS, TILE, TILE), np.float32)
    inv = np.zeros((2, TILE, POOL_W), np.float32)
    t = np.arange(TILE)
    for kind, n in ((0, TILE), (1, GRID_W)):
        pos, base = t % n, (t // n) * n
        for gi, w in enumerate(POOL_WINDOWS):
            lo = np.clip(pos - w // 2, 0, n)
            hi = np.clip(pos + w - w // 2, 0, n)
            s = t[None, :]
            a[kind, gi] = ((s >= (base + lo)[:, None]) & (s < (base + hi)[:, None])).astype(np.float32)
            inv[kind, :, gi * POOL_GROUP_DIM:(gi + 1) * POOL_GROUP_DIM] = (1.0 / (hi - lo))[:, None]
    return a, inv


def _ada_kernel(c_ref, w_ref, b_ref, o_ref):
    o_ref[0] = jnp.dot(_silu(c_ref[...]), w_ref[0], preferred_element_type=F32,
                       precision=lax.Precision.HIGHEST) + b_ref[0]


def _ada_table(cvec, w_ada, b_ada):
    depth, d, n = w_ada.shape
    nb = 4
    rows = cvec.shape[0]
    return pl.pallas_call(
        _ada_kernel,
        out_shape=jax.ShapeDtypeStruct((depth, rows, n), F32),
        grid=(depth, nb),
        in_specs=[pl.BlockSpec((rows, d), lambda l, j: (0, 0)),
                  pl.BlockSpec((1, d, n // nb), lambda l, j: (l, 0, j)),
                  pl.BlockSpec((1, 1, n // nb), lambda l, j: (l, 0, j))],
        out_specs=pl.BlockSpec((1, rows, n // nb), lambda l, j: (l, 0, j)),
        compiler_params=pltpu.CompilerParams(dimension_semantics=("arbitrary", "arbitrary"),
                                             vmem_limit_bytes=VMEM_LIMIT),
        name="ada_table",
    )(cvec, w_ada, b_ada.reshape(depth, 1, n))


def _reference_rows(b, lvl, backward):
    w = 1 << lvl
    pm = w if backward else w - 1
    if w >= 4:
        n = TILE // (2 * w)
        row = b.reshape(n, 2 * w, HEAD_DIM)[:, pm:pm + 1, :]
        return jnp.broadcast_to(row, (n, 2 * w, HEAD_DIM)).reshape(TILE, HEAD_DIM)
    pos = lax.broadcasted_iota(jnp.int32, (TILE, HEAD_DIM), 0) & (2 * w - 1)
    out = b
    for p in range(2 * w):
        if p == pm:
            continue
        out = jnp.where(pos == p, pltpu.roll(b, (p - pm) % TILE, axis=0), out)
    return out


def _combine(ys_ref, posw):
    lane = lax.broadcasted_iota(jnp.int32, (TILE, TILE), 1).astype(F32)
    acc = None
    for rb in range(SORT_ROWS // TILE):
        pw = jnp.zeros((TILE, TILE), F32)
        for k in range(TOP_K):
            pw = pw + jnp.where(lane == posw[:, k:k + 1] - float(rb * TILE), posw[:, 8 + k:9 + k], 0.0)
        part = _dot(pw.astype(BF16), ys_ref[rb * TILE:(rb + 1) * TILE, :])
        acc = part if acc is None else acc + part
    return acc


def _mixer_in_kernel(has_prev, *refs):
    if has_prev:
        xb_ref, ys_ref, posw_ref, modp_ref = refs[:4]
        refs = refs[4:]
    else:
        x_ref = refs[0]
        refs = refs[1:]
    (mod_ref, nw_ref, win_ref, lb_ref, tri_ref, lvl_ref, pa_ref, pinv_ref, wp_ref, ps_ref) = refs[:10]
    outs = refs[10:]
    if has_prev:
        oi_ref, qe_ref, ut_ref, a_ref, og_ref, po_ref, xo_ref = outs
        x = xb_ref[0] + modp_ref[0, 0][5:6] * _combine(ys_ref, posw_ref[0])
        xo_ref[0] = x
    else:
        oi_ref, qe_ref, ut_ref, a_ref, og_ref, po_ref = outs
        x = x_ref[0]

    m = mod_ref[0, 0]
    hx = (_rmsnorm(x, nw_ref[...]) * (1.0 + m[1:2]) + m[0:1]).astype(BF16)

    def proj(col):
        return _dot(hx, win_ref[:, col * HGRN_W:(col + 1) * HGRN_W])

    v = proj(2).astype(BF16)
    q = _silu(proj(3)) * (HEAD_DIM ** -0.5)
    og_ref[0] = _silu(proj(4)).astype(BF16)

    p = proj(5)
    pb = p.astype(BF16)
    pinv = pinv_ref[0]
    for g in range(POOL_GROUPS):
        sl = slice(g * POOL_GROUP_DIM, (g + 1) * POOL_GROUP_DIM)
        pooled = _dot(pa_ref[0, g], pb[:, sl]) * pinv[:, sl] - p[:, sl]
        po_ref[0, :, sl] = (_dot(pooled.astype(BF16), wp_ref[g]) * ps_ref[:, sl]).astype(BF16)

    ks, bs = [], []
    for d in range(2):
        z = proj(d)
        lb = lb_ref[d:d + 1, :]
        kk = (1.0 - lb) * _sigmoid(-z)
        g = jnp.log(jnp.maximum(lb, LB_FLOOR) + (1.0 - lb) * _sigmoid(z))
        hi, mid, lo = _split3(g)
        tri = tri_ref[d]
        b = _dot(tri, hi) + _dot(tri, mid) + _dot(tri, lo)
        b_tot = b[TILE - 1:TILE, :] if d == 0 else b[0:1, :]
        qe_ref[0, :, d * HGRN_W:(d + 1) * HGRN_W] = (q * jnp.exp(b)).astype(BF16)
        k_end = (kk * jnp.exp(b_tot - b)).astype(BF16)
        a_tot = jnp.exp(b_tot)
        for h in range(N_HEADS):
            sl = slice(h * HEAD_DIM, (h + 1) * HEAD_DIM)
            ut_ref[0, 0, d, h] = _dot_tn(v[:, sl], k_end[:, sl])
            a_ref[0, 0, d, h] = a_tot[:, sl]
        ks.append(kk)
        bs.append(b)

    rows = lax.broadcasted_iota(jnp.int32, (TILE, HEAD_DIM), 0)
    eye = (lax.broadcasted_iota(jnp.int32, (TILE, TILE), 0)
           == lax.broadcasted_iota(jnp.int32, (TILE, TILE), 1))
    for h in range(N_HEADS):
        sl = slice(h * HEAD_DIM, (h + 1) * HEAD_DIM)
        qh = q[:, sl]
        diag = jnp.sum(qh * (ks[0][:, sl] + ks[1][:, sl]), axis=-1, keepdims=True)
        scores = jnp.where(eye, diag, 0.0)
        for d in range(2):
            kh, bh = ks[d][:, sl], bs[d][:, sl]
            lmat = lvl_ref[d]
            for lvl in range(N_LEVELS):
                upper = ((rows >> lvl) & 1) == 1
                is_t = upper if d == 0 else jnp.logical_not(upper)
                ref_b = _reference_rows(bh, lvl, d == 1)
                arg = jnp.where(is_t, bh - ref_b, ref_b - bh)
                xm = (jnp.where(is_t, qh, kh) * jnp.exp(arg)).astype(BF16)
                scores = jnp.where(lmat == lvl, _dot_nt(xm, xm), scores)
        oi_ref[0, :, sl] = _dot(scores.astype(BF16), v[:, sl])


def _mixer_in(has_prev, xs, modx, nw, win, lb, consts, wp, ps):
    tri, lvlm, pa, pinv = consts
    bsz, ntok, d = xs[0].shape
    nt = ntok // TILE
    tok = lambda w: pl.BlockSpec((1, TILE, w), lambda b, j: (b, j, 0))
    modspec = pl.BlockSpec((1, 1, 6, d), lambda b, j: (b, jnp.minimum(j, 1), 0, 0))
    full = lambda a: pl.BlockSpec(a.shape, lambda b, j: (0,) * a.ndim)
    sorted_rows = pl.BlockSpec((SORT_ROWS, d), lambda b, j: (b * nt + j, 0))
    in_specs = ([tok(d), sorted_rows, tok(128), modspec] if has_prev else [tok(d)]) + [
        modspec, full(nw), full(win), full(lb), full(tri), full(lvlm),
        pl.BlockSpec((1,) + pa.shape[1:], lambda b, j: (jnp.minimum(j, 1), 0, 0, 0)),
        pl.BlockSpec((1,) + pinv.shape[1:], lambda b, j: (jnp.minimum(j, 1), 0, 0)),
        full(wp), full(ps)]
    out_shape = [jax.ShapeDtypeStruct((bsz, ntok, HGRN_W), F32),
                 jax.ShapeDtypeStruct((bsz, ntok, 2 * HGRN_W), BF16),
                 jax.ShapeDtypeStruct((bsz, nt, 2, N_HEADS, HEAD_DIM, HEAD_DIM), F32),
                 jax.ShapeDtypeStruct((bsz, nt, 2, N_HEADS, 1, HEAD_DIM), F32),
                 jax.ShapeDtypeStruct((bsz, ntok, HGRN_W), BF16),
                 jax.ShapeDtypeStruct((bsz, ntok, POOL_W), BF16)]
    out_specs = [tok(HGRN_W), tok(2 * HGRN_W),
                 pl.BlockSpec((1, 1, 2, N_HEADS, HEAD_DIM, HEAD_DIM), lambda b, j: (b, j, 0, 0, 0, 0)),
                 pl.BlockSpec((1, 1, 2, N_HEADS, 1, HEAD_DIM), lambda b, j: (b, j, 0, 0, 0, 0)),
                 tok(HGRN_W), tok(POOL_W)]
    if has_prev:
        out_shape.append(jax.ShapeDtypeStruct((bsz, ntok, d), F32))
        out_specs.append(tok(d))
    args = list(xs) + [modx, nw, win, lb, tri, lvlm, pa, pinv, wp, ps]
    return pl.pallas_call(
        functools.partial(_mixer_in_kernel, has_prev),
        out_shape=out_shape, grid=(bsz, nt), in_specs=in_specs, out_specs=out_specs,
        compiler_params=pltpu.CompilerParams(dimension_semantics=("parallel", "parallel"),
                                             vmem_limit_bytes=VMEM_LIMIT),
        name="mixer_in",
    )(*args)


def _scan_kernel(utf_ref, af_ref, utb_ref, ab_ref, sf_ref, sb_ref, accf, accb):
    @pl.when(pl.program_id(1) == 0)
    def _():
        accf[...] = jnp.zeros_like(accf)
        accb[...] = jnp.zeros_like(accb)

    sf_ref[0, 0] = accf[...]
    sb_ref[0, 0] = accb[...]
    accf[...] = af_ref[0, 0, 0] * accf[...] + utf_ref[0, 0, 0]
    accb[...] = ab_ref[0, 0, 0] * accb[...] + utb_ref[0, 0, 0]


def _scan_state(ut, a):
    bsz, nt = ut.shape[:2]
    bwd = lambda s: jnp.where(s == 0, 0, nt - s)
    st = (N_HEADS, HEAD_DIM, HEAD_DIM)
    ut_spec = lambda d, o: pl.BlockSpec((1, 1, 1) + st, lambda b, s: (b, o(s), d, 0, 0, 0))
    a_spec = lambda d, o: pl.BlockSpec((1, 1, 1, N_HEADS, 1, HEAD_DIM), lambda b, s: (b, o(s), d, 0, 0, 0))
    ident = lambda s: s
    return pl.pallas_call(
        _scan_kernel,
        out_shape=[jax.ShapeDtypeStruct((bsz, nt) + st, F32)] * 2,
        grid=(bsz, nt),
        in_specs=[ut_spec(0, ident), a_spec(0, ident), ut_spec(1, bwd), a_spec(1, bwd)],
        out_specs=[pl.BlockSpec((1, 1) + st, lambda b, s: (b, s, 0, 0, 0)),
                   pl.BlockSpec((1, 1) + st, lambda b, s: (b, bwd(s), 0, 0, 0))],
        scratch_shapes=[pltpu.VMEM(st, F32), pltpu.VMEM(st, F32)],
        compiler_params=pltpu.CompilerParams(dimension_semantics=("parallel", "arbitrary"),
                                             vmem_limit_bytes=VMEM_LIMIT),
        name="scan_state",
    )(ut, a, ut, a)


def _route(scores, bias):
    t, e = scores.shape
    lane = lax.broadcasted_iota(jnp.int32, (t, e), 1)
    lane_f = lane.astype(F32)
    grp_f = (lane >> 3).astype(F32)
    big = float(4 * e)
    biased = scores + bias

    def first_max(x, idx_f):
        mx = jnp.max(x, axis=1, keepdims=True)
        return mx, jnp.min(jnp.where(x == mx, idx_f, big), axis=1, keepdims=True)

    grp_score = jnp.zeros_like(biased)
    for g in range(N_EXPERT_GROUPS):
        in_g = grp_f == float(g)
        xg = jnp.where(in_g, biased, NEG_INF)
        m1, i1 = first_max(xg, lane_f)
        m2 = jnp.max(jnp.where(lane_f == i1, NEG_INF, xg), axis=1, keepdims=True)
        grp_score = jnp.where(in_g, m1 + m2, grp_score)
    keep = jnp.zeros_like(biased)
    work = grp_score
    for _ in range(TOPK_GROUPS):
        _, gi = first_max(work, grp_f)
        hit = grp_f == gi
        keep = jnp.where(hit, 1.0, keep)
        work = jnp.where(hit, NEG_INF, work)
    work = jnp.where(keep > 0.0, biased, MASK_SCORE)
    sel = jnp.zeros_like(biased)
    hits = []
    for _ in range(TOP_K):
        _, ei = first_max(work, lane_f)
        hit = lane_f == ei
        hits.append(hit)
        sel = jnp.where(hit, 1.0, sel)
        work = jnp.where(hit, NEG_INF, work)
    gate = jnp.where(sel > 0.0, scores, 0.0)
    return gate / jnp.sum(gate, axis=1, keepdims=True) * ROUTED_SCALE, sel, hits


def _sort_rows(gates, sel, hits, h2b, tri_lower, xs_ref, posw_ref, cnt_ref):
    t, e = sel.shape
    incl = _dot(tri_lower, sel.astype(BF16))
    cnt = incl[t - 1:t, :]
    ngran = jnp.floor((cnt + float(GRAN - 1)) * (1.0 / GRAN))
    ei = lax.broadcasted_iota(jnp.int32, (e, e), 0)
    ej = lax.broadcasted_iota(jnp.int32, (e, e), 1)
    before = jnp.where(ei < ej, 1.0, 0.0).astype(BF16)
    run_start = _dot(jnp.broadcast_to(ngran, (8, e)).astype(BF16), before)[0:1, :] * float(GRAN)
    slot = run_start + incl - sel
    cnt_ref[0, 0] = jnp.broadcast_to(cnt, (8, e))

    lane = lax.broadcasted_iota(jnp.int32, (t, 128), 1)
    posw = jnp.where(lane < 8, -1.0, 0.0)
    for k, hit in enumerate(hits):
        pos_k = jnp.sum(jnp.where(hit, slot, 0.0), axis=1, keepdims=True)
        w_k = jnp.sum(jnp.where(hit, gates, 0.0), axis=1, keepdims=True)
        posw = jnp.where(lane == k, pos_k, jnp.where(lane == 8 + k, w_k, posw))
    posw_ref[0] = posw

    pos_t = posw.T
    row = lax.broadcasted_iota(jnp.int32, (TILE, t), 0).astype(F32)
    for rb in range(SORT_ROWS // TILE):
        onehot = jnp.zeros((TILE, t), F32)
        for k in range(TOP_K):
            onehot = onehot + jnp.where(row == pos_t[k:k + 1, :] - float(rb * TILE), 1.0, 0.0)
        xs_ref[rb * TILE:(rb + 1) * TILE, :] = _dot(onehot.astype(BF16), h2b).astype(BF16)


def _mixer_out_kernel(x_ref, oi_ref, qe_ref, sf_ref, sb_ref, og_ref, po_ref, mod_ref, gw_ref, wo_ref,
                      nw2_ref, wr_ref, rb_ref, sw1_ref, sw3_ref, sw2_ref, tri_ref,
                      xb_ref, xs_ref, posw_ref, cnt_ref):
    m = mod_ref[0, 0]
    parts = []
    for h in range(N_HEADS):
        sl = slice(h * HEAD_DIM, (h + 1) * HEAD_DIM)
        o = (oi_ref[0, :, sl]
             + _dot_nt(qe_ref[0, :, sl], sf_ref[0, 0, h].astype(BF16))
             + _dot_nt(qe_ref[0, :, HGRN_W + h * HEAD_DIM:HGRN_W + (h + 1) * HEAD_DIM],
                       sb_ref[0, 0, h].astype(BF16)))
        y = _rmsnorm(o, gw_ref[...]) * og_ref[0, :, sl].astype(F32)
        parts.append(y.astype(BF16))
    mix = jnp.concatenate(parts + [po_ref[0]], axis=-1)
    x1 = x_ref[0] + m[2:3] * _dot(mix, wo_ref[...])
    h2 = _rmsnorm(x1, nw2_ref[...]) * (1.0 + m[4:5]) + m[3:4]
    h2b = h2.astype(BF16)
    shared = _dot((_silu(_dot(h2b, sw1_ref[...])) * _dot(h2b, sw3_ref[...])).astype(BF16), sw2_ref[...])
    xb_ref[0] = x1 + m[5:6] * shared
    logits = jnp.dot(h2, wr_ref[...], preferred_element_type=F32, precision=lax.Precision.HIGHEST)
    gates, sel, hits = _route(_sigmoid(logits), rb_ref[...])
    _sort_rows(gates, sel, hits, h2b, tri_ref[0], xs_ref, posw_ref, cnt_ref)


def _mixer_out(x, oi, qe, sf, sb, og, po, modx, gw, wo, nw2, wr, rb, sw1, sw3, sw2, tri):
    bsz, ntok, d = x.shape
    nt = ntok // TILE
    tok = lambda w: pl.BlockSpec((1, TILE, w), lambda b, j: (b, j, 0))
    st = pl.BlockSpec((1, 1, N_HEADS, HEAD_DIM, HEAD_DIM), lambda b, j: (b, j, 0, 0, 0))
    full = lambda a: pl.BlockSpec(a.shape, lambda b, j: (0,) * a.ndim)
    return pl.pallas_call(
        _mixer_out_kernel,
        out_shape=[jax.ShapeDtypeStruct((bsz, ntok, d), F32),
                   jax.ShapeDtypeStruct((bsz * nt * SORT_ROWS, d), BF16),
                   jax.ShapeDtypeStruct((bsz, ntok, 128), F32),
                   jax.ShapeDtypeStruct((bsz, nt, 8, N_EXPERTS), F32)],
        grid=(bsz, nt),
        in_specs=[tok(d), tok(HGRN_W), tok(2 * HGRN_W), st, st, tok(HGRN_W), tok(POOL_W),
                  pl.BlockSpec((1, 1, 6, d), lambda b, j: (b, jnp.minimum(j, 1), 0, 0)),
                  full(gw), full(wo), full(nw2), full(wr), full(rb), full(sw1), full(sw3), full(sw2),
                  full(tri)],
        out_specs=[tok(d), pl.BlockSpec((SORT_ROWS, d), lambda b, j: (b * nt + j, 0)), tok(128),
                   pl.BlockSpec((1, 1, 8, N_EXPERTS), lambda b, j: (b, j, 0, 0))],
        compiler_params=pltpu.CompilerParams(dimension_semantics=("parallel", "parallel"),
                                             vmem_limit_bytes=VMEM_LIMIT),
        name="mixer_out",
    )(x, oi, qe, sf, sb, og, po, modx, gw, wo, nw2, wr, rb, sw1, sw3, sw2, tri)


def _moe_plan(cnt, nb_max):
    ntile, ne = cnt.shape
    ngran = (cnt + GRAN - 1) // GRAN
    run_start = jnp.cumsum(ngran, axis=1) - ngran
    ngran_t = ngran.T
    per_e = ngran_t.sum(axis=1)
    nblk = (per_e + GRAN_PER_BLOCK - 1) // GRAN_PER_BLOCK
    blk_end = jnp.cumsum(nblk)
    blk_start = blk_end - nblk
    nb = blk_end[-1]
    bidx = jnp.arange(nb_max, dtype=jnp.int32)
    be = jnp.minimum(jnp.searchsorted(blk_end, bidx, side="right"), ne - 1).astype(jnp.int32)
    be = jnp.where(bidx < nb, be, be[jnp.maximum(nb - 1, 0)])
    q0 = (bidx - blk_start[be]) * GRAN_PER_BLOCK
    nvalid = jnp.where(bidx < nb, jnp.clip(per_e[be] - q0, 0, GRAN_PER_BLOCK), 0)
    flat = ngran_t.reshape(-1)
    run_end = jnp.cumsum(flat)
    slot = jnp.arange(GRAN_PER_BLOCK, dtype=jnp.int32)[None, :]
    gq = (jnp.cumsum(per_e) - per_e)[be][:, None] + q0[:, None] + slot
    run = jnp.minimum(jnp.searchsorted(run_end, gq.reshape(-1), side="right"), ne * ntile - 1)
    run = run.reshape(gq.shape)
    tile = run % ntile
    src = tile * GRAN_PER_TILE + run_start[tile, be[:, None]] + gq - (run_end - flat)[run]
    src = jnp.where(slot < nvalid[:, None], src, 0)
    return (nvalid.astype(jnp.int32), be, nb.reshape(1).astype(jnp.int32),
            src.reshape(-1).astype(jnp.int32))


def _moe_ffn_kernel(nv_ref, be_ref, nb_ref, src_ref, xs_hbm, w1_ref, w3_ref, w2_ref, ys_hbm,
                    xin, yout, sem_in, sem_out):
    b = pl.program_id(0)
    nb = nb_ref[0]
    slot = lax.rem(b, 2)

    def granule_copies(blk, sl, inbound, wait):
        nv = nv_ref[blk]
        for i in range(GRAN_PER_BLOCK):
            @pl.when(i < nv)
            def _():
                row = pl.multiple_of(src_ref[blk * GRAN_PER_BLOCK + i] * GRAN, GRAN)
                if inbound:
                    cp = pltpu.make_async_copy(xs_hbm.at[pl.ds(row, GRAN), :],
                                               xin.at[sl, pl.ds(i * GRAN, GRAN), :], sem_in.at[sl])
                else:
                    cp = pltpu.make_async_copy(yout.at[sl, pl.ds(i * GRAN, GRAN), :],
                                               ys_hbm.at[pl.ds(row, GRAN), :], sem_out.at[sl])
                if wait:
                    cp.wait()
                else:
                    cp.start()

    @pl.when(b < nb)
    def _():
        @pl.when(b == 0)
        def _():
            xin[...] = jnp.zeros_like(xin)
            granule_copies(b, slot, True, False)

        @pl.when(b + 1 < nb)
        def _():
            granule_copies(b + 1, 1 - slot, True, False)

        granule_copies(b, slot, True, True)

        @pl.when(b >= 2)
        def _():
            granule_copies(b - 2, slot, False, True)

        x = xin[slot]
        hid = (_silu(_dot(x, w1_ref[0])) * _dot(x, w3_ref[0])).astype(BF16)
        yout[slot] = _dot(hid, w2_ref[0]).astype(BF16)
        granule_copies(b, slot, False, False)

        @pl.when(b == nb - 1)
        def _():
            granule_copies(b, slot, False, True)

            @pl.when(b >= 1)
            def _():
                granule_copies(b - 1, 1 - slot, False, True)


def _moe_ffn(xs, plan, w1, w3, w2, nb_max):
    nrows, d = xs.shape
    ne, _, f = w1.shape
    rows = GRAN_PER_BLOCK * GRAN
    wspec = lambda shape: pl.BlockSpec((1,) + shape, lambda b, nv, be, nb, src: (be[b], 0, 0))
    return pl.pallas_call(
        _moe_ffn_kernel,
        out_shape=jax.ShapeDtypeStruct((nrows, d), BF16),
        grid_spec=pltpu.PrefetchScalarGridSpec(
            num_scalar_prefetch=4, grid=(nb_max,),
            in_specs=[pl.BlockSpec(memory_space=pl.ANY), wspec((d, f)), wspec((d, f)), wspec((f, d))],
            out_specs=pl.BlockSpec(memory_space=pl.ANY),
            scratch_shapes=[pltpu.VMEM((2, rows, d), BF16), pltpu.VMEM((2, rows, d), BF16),
                            pltpu.SemaphoreType.DMA((2,)), pltpu.SemaphoreType.DMA((2,))]),
        input_output_aliases={4: 0},
        compiler_params=pltpu.CompilerParams(dimension_semantics=("arbitrary",),
                                             vmem_limit_bytes=VMEM_LIMIT),
        name="moe_ffn",
    )(*plan, xs, w1, w3, w2)


def _final_kernel(xb_ref, ys_ref, posw_ref, mod_ref, nw_ref, o_ref):
    x = xb_ref[0] + mod_ref[0, 0][5:6] * _combine(ys_ref, posw_ref[0])
    o_ref[0] = _rmsnorm(x, nw_ref[...])


def _final(xb, ys, posw, modx, nw, n_ctx_tiles):
    bsz, ntok, d = xb.shape
    nt_all = ntok // TILE
    nt = nt_all - n_ctx_tiles
    lat = lambda w: pl.BlockSpec((1, TILE, w), lambda b, j: (b, j + n_ctx_tiles, 0))
    return pl.pallas_call(
        _final_kernel,
        out_shape=jax.ShapeDtypeStruct((bsz, nt * TILE, d), F32),
        grid=(bsz, nt),
        in_specs=[lat(d), pl.BlockSpec((SORT_ROWS, d), lambda b, j: (b * nt_all + j + n_ctx_tiles, 0)),
                  lat(128), pl.BlockSpec((1, 1, 6, d), lambda b, j: (b, 1, 0, 0)),
                  pl.BlockSpec(nw.shape, lambda b, j: (0, 0))],
        out_specs=pl.BlockSpec((1, TILE, d), lambda b, j: (b, j, 0)),
        compiler_params=pltpu.CompilerParams(dimension_semantics=("parallel", "parallel"),
                                             vmem_limit_bytes=VMEM_LIMIT),
        name="final_norm",
    )(xb, ys, posw, modx, nw)


def kernel(x, c, ctx, c_ctx, norm1_w, norm2_w, w_ada, b_ada, w_in, hgrn_lb_logits, hgrn_gnorm_w, w_pool,
           pool_scale, w_out, w_router, router_bias, w1_experts, w3_experts, w2_experts, w1_shared,
           w3_shared, w2_shared, final_norm_w):
    bsz, seq_len, d = x.shape
    ctx_len = ctx.shape[1]
    depth = w_ada.shape[0]
    assert ctx_len == TILE and seq_len % TILE == 0 and seq_len % GRID_W == 0
    ntok = ctx_len + seq_len
    ntile = bsz * ntok // TILE
    nb_max = -(-ntile * MAX_GRAN_PER_TILE // GRAN_PER_BLOCK) + N_EXPERTS

    lf, lbm = _level_matrices()
    lower, upper = _tri_matrices()
    pa, pinv = _pool_matrices()
    consts = (jnp.asarray(np.stack([lower, upper]), BF16), jnp.asarray(np.stack([lf, lbm])),
              jnp.asarray(pa, BF16), jnp.asarray(pinv))

    p_lb = jax.nn.softmax(hgrn_lb_logits.astype(F32), axis=0)
    lb_all = jnp.cumsum(p_lb, axis=0) - p_lb[0:1]

    rows = 16
    cvec = jnp.concatenate([c, c_ctx[None, :], jnp.zeros((rows - bsz - 1, d), F32)], axis=0)
    mod = _ada_table(cvec, w_ada, b_ada)
    mod_lat = mod[:, :bsz].reshape(depth, bsz, 6, d)
    mod_ctx = jnp.broadcast_to(mod[:, bsz].reshape(depth, 1, 6, d), (depth, bsz, 6, d))
    modx = jnp.stack([mod_ctx, mod_lat], axis=2)

    xs = (jnp.concatenate([ctx, x], axis=1),)
    for l in range(depth):
        has_prev = l > 0
        outs = _mixer_in(has_prev, xs + ((modx[l - 1],) if has_prev else ()), modx[l], norm1_w[l][None, :],
                         w_in[l].astype(BF16), lb_all[l], consts, w_pool[l].astype(BF16),
                         pool_scale[l][None, :])
        oi, qe, ut, a, og, po = outs[:6]
        x_cur = outs[6] if has_prev else xs[0]
        sf, sb = _scan_state(ut, a)
        xb, rows_sorted, posw, cnt = _mixer_out(
            x_cur, oi, qe, sf, sb, og, po, modx[l], hgrn_gnorm_w[l][None, :], w_out[l].astype(BF16),
            norm2_w[l][None, :], w_router[l], router_bias[l][None, :], w1_shared[l].astype(BF16),
            w3_shared[l].astype(BF16), w2_shared[l].astype(BF16), consts[0])
        plan = _moe_plan(cnt[:, :, 0, :].reshape(ntile, N_EXPERTS).astype(jnp.int32), nb_max)
        ys = _moe_ffn(rows_sorted, plan, w1_experts[l].astype(BF16), w3_experts[l].astype(BF16),
                      w2_experts[l].astype(BF16), nb_max)
        xs = (xb, ys, posw)
    return _final(xs[0], xs[1], xs[2], modx[depth - 1], final_norm_w[None, :], ctx_len // TILE)
```

```python
import functools

import numpy as np
import jax
import jax.numpy as jnp
from jax import lax
from jax.experimental import pallas as pl
from jax.experimental.pallas import tpu as pltpu

F32 = jnp.float32
BF16 = jnp.bfloat16

EPS = 1e-6
LB_FLOOR = 1e-30
MASK_SCORE = -1e4
GRID_W = 64
HEAD_DIM = 128
N_HEADS = 4
HGRN_W = N_HEADS * HEAD_DIM
POOL_WINDOWS = (2, 4, 8, 16)
POOL_GROUPS = len(POOL_WINDOWS)
POOL_GROUP_DIM = 128
POOL_W = POOL_GROUPS * POOL_GROUP_DIM
N_EXPERTS = 64
TOP_K = 6
N_EXPERT_GROUPS = 8
EXPERTS_PER_GROUP = N_EXPERTS // N_EXPERT_GROUPS
TOPK_GROUPS = 4
ROUTED_SCALE = 2.5

TILE = 256
N_LEVELS = 8
GRAN = 16
MAX_GRAN_PER_TILE = (TILE * TOP_K + N_EXPERTS * (GRAN - 1)) // GRAN
GRAN_PER_TILE = -(-MAX_GRAN_PER_TILE * GRAN // TILE) * TILE // GRAN
SORT_ROWS = GRAN_PER_TILE * GRAN
GRAN_PER_BLOCK = 32
VMEM_LIMIT = 56 * 1024 * 1024
NEG_INF = float("-inf")


def _sigmoid(z):
    return 1.0 / (1.0 + jnp.exp(-z))


def _silu(z):
    return z * _sigmoid(z)


def _dot(a, b):
    return jnp.dot(a, b, preferred_element_type=F32)


def _dot_nt(a, b):
    return lax.dot_general(a, b, (((1,), (1,)), ((), ())), preferred_element_type=F32)


def _dot_tn(a, b):
    return lax.dot_general(a, b, (((0,), (0,)), ((), ())), preferred_element_type=F32)


def _split3(g):
    hi = g.astype(BF16)
    r1 = g - hi.astype(F32)
    mid = r1.astype(BF16)
    lo = (r1 - mid.astype(F32)).astype(BF16)
    return hi, mid, lo


def _rmsnorm(x, w):
    return x * lax.rsqrt(jnp.mean(x * x, axis=-1, keepdims=True) + EPS) * w


def _level_matrices():
    t = np.arange(TILE)
    xor = t[:, None] ^ t[None, :]
    lvl = np.floor(np.log2(np.maximum(xor, 1))).astype(np.int32)
    lf = np.where(t[:, None] > t[None, :], lvl, -1).astype(np.int32)
    return lf, lf.T.copy()


def _tri_matrices():
    t = np.arange(TILE)
    lower = (t[:, None] >= t[None, :]).astype(np.float32)
    return lower, lower.T.copy()


def _pool_matrices():
    a = np.zeros((2, POOL_GROUPS, TILE, TILE), np.float32)
    inv = np.zeros((2, TILE, POOL_W), np.float32)
    t = np.arange(TILE)
    for kind, n in ((0, TILE), (1, GRID_W)):
        pos, base = t % n, (t // n) * n
        for gi, w in enumerate(POOL_WINDOWS):
            lo = np.clip(pos - w // 2, 0, n)
            hi = np.clip(pos + w - w // 2, 0, n)
            s = t[None, :]
            a[kind, gi] = ((s >= (base + lo)[:, None]) & (s < (base + hi)[:, None])).astype(np.float32)
            inv[kind, :, gi * POOL_GROUP_DIM:(gi + 1) * POOL_GROUP_DIM] = (1.0 / (hi - lo))[:, None]
    return a, inv


def _ada_kernel(c_ref, w_ref, b_ref, o_ref):
    o_ref[0] = jnp.dot(_silu(c_ref[...]), w_ref[0], preferred_element_type=F32,
                       precision=lax.Precision.HIGHEST) + b_ref[0]


def _ada_table(cvec, w_ada, b_ada):
    depth, d, n = w_ada.shape
    nb = 4
    rows = cvec.shape[0]
    return pl.pallas_call(
        _ada_kernel,
        out_shape=jax.ShapeDtypeStruct((depth, rows, n), F32),
        grid=(depth, nb),
        in_specs=[pl.BlockSpec((rows, d), lambda l, j: (0, 0)),
                  pl.BlockSpec((1, d, n // nb), lambda l, j: (l, 0, j)),
                  pl.BlockSpec((1, 1, n // nb), lambda l, j: (l, 0, j))],
        out_specs=pl.BlockSpec((1, rows, n // nb), lambda l, j: (l, 0, j)),
        compiler_params=pltpu.CompilerParams(dimension_semantics=("arbitrary", "arbitrary"),
                                             vmem_limit_bytes=VMEM_LIMIT),
        name="ada_table",
    )(cvec, w_ada, b_ada.reshape(depth, 1, n))


def _reference_rows(b, lvl, backward):
    w = 1 << lvl
    pm = w if backward else w - 1
    if w >= 4:
        n = TILE // (2 * w)
        row = b.reshape(n, 2 * w, HEAD_DIM)[:, pm:pm + 1, :]
        return jnp.broadcast_to(row, (n, 2 * w, HEAD_DIM)).reshape(TILE, HEAD_DIM)
    pos = lax.broadcasted_iota(jnp.int32, (TILE, HEAD_DIM), 0) & (2 * w - 1)
    out = b
    for p in range(2 * w):
        if p == pm:
            continue
        out = jnp.where(pos == p, pltpu.roll(b, (p - pm) % TILE, axis=0), out)
    return out


def _combine(ys_ref, posw):
    lane = lax.broadcasted_iota(jnp.int32, (TILE, TILE), 1).astype(F32)
    acc = None
    for rb in range(SORT_ROWS // TILE):
        pw = jnp.zeros((TILE, TILE), F32)
        for k in range(TOP_K):
            pw = pw + jnp.where(lane == posw[:, k:k + 1] - float(rb * TILE), posw[:, 8 + k:9 + k], 0.0)
        part = _dot(pw.astype(BF16), ys_ref[rb * TILE:(rb + 1) * TILE, :])
        acc = part if acc is None else acc + part
    return acc


def _mixer_in_kernel(has_prev, *refs):
    if has_prev:
        xb_ref, ys_ref, posw_ref, modp_ref = refs[:4]
        refs = refs[4:]
    else:
        x_ref = refs[0]
        refs = refs[1:]
    (mod_ref, nw_ref, win_ref, lb_ref, tri_ref, lvl_ref, pa_ref, pinv_ref, wp_ref, ps_ref) = refs[:10]
    outs = refs[10:]
    if has_prev:
        oi_ref, qe_ref, ut_ref, a_ref, og_ref, po_ref, xo_ref = outs
        x = xb_ref[0] + modp_ref[0, 0][5:6] * _combine(ys_ref, posw_ref[0])
        xo_ref[0] = x
    else:
        oi_ref, qe_ref, ut_ref, a_ref, og_ref, po_ref = outs
        x = x_ref[0]

    m = mod_ref[0, 0]
    hx = (_rmsnorm(x, nw_ref[...]) * (1.0 + m[1:2]) + m[0:1]).astype(BF16)

    def proj(col):
        return _dot(hx, win_ref[:, col * HGRN_W:(col + 1) * HGRN_W])

    v = proj(2).astype(BF16)
    q = _silu(proj(3)) * (HEAD_DIM ** -0.5)
    og_ref[0] = _silu(proj(4)).astype(BF16)

    p = proj(5)
    pb = p.astype(BF16)
    pinv = pinv_ref[0]
    for g in range(POOL_GROUPS):
        sl = slice(g * POOL_GROUP_DIM, (g + 1) * POOL_GROUP_DIM)
        pooled = _dot(pa_ref[0, g], pb[:, sl]) * pinv[:, sl] - p[:, sl]
        po_ref[0, :, sl] = (_dot(pooled.astype(BF16), wp_ref[g]) * ps_ref[:, sl]).astype(BF16)

    ks, bs = [], []
    for d in range(2):
        z = proj(d)
        lb = lb_ref[d:d + 1, :]
        kk = (1.0 - lb) * _sigmoid(-z)
        g = jnp.log(jnp.maximum(lb, LB_FLOOR) + (1.0 - lb) * _sigmoid(z))
        hi, mid, lo = _split3(g)
        tri = tri_ref[d]
        b = _dot(tri, hi) + _dot(tri, mid) + _dot(tri, lo)
        b_tot = b[TILE - 1:TILE, :] if d == 0 else b[0:1, :]
        qe_ref[0, :, d * HGRN_W:(d + 1) * HGRN_W] = (q * jnp.exp(b)).astype(BF16)
        k_end = (kk * jnp.exp(b_tot - b)).astype(BF16)
        a_tot = jnp.exp(b_tot)
        for h in range(N_HEADS):
            sl = slice(h * HEAD_DIM, (h + 1) * HEAD_DIM)
            ut_ref[0, 0, d, h] = _dot_tn(v[:, sl], k_end[:, sl])
            a_ref[0, 0, d, h] = a_tot[:, sl]
        ks.append(kk)
        bs.append(b)

    rows = lax.broadcasted_iota(jnp.int32, (TILE, HEAD_DIM), 0)
    eye = (lax.broadcasted_iota(jnp.int32, (TILE, TILE), 0)
           == lax.broadcasted_iota(jnp.int32, (TILE, TILE), 1))
    for h in range(N_HEADS):
        sl = slice(h * HEAD_DIM, (h + 1) * HEAD_DIM)
        qh = q[:, sl]
        diag = jnp.sum(qh * (ks[0][:, sl] + ks[1][:, sl]), axis=-1, keepdims=True)
        scores = jnp.where(eye, diag, 0.0)
        for d in range(2):
            kh, bh = ks[d][:, sl], bs[d][:, sl]
            lmat = lvl_ref[d]
            for lvl in range(N_LEVELS):
                upper = ((rows >> lvl) & 1) == 1
                is_t = upper if d == 0 else jnp.logical_not(upper)
                ref_b = _reference_rows(bh, lvl, d == 1)
                arg = jnp.where(is_t, bh - ref_b, ref_b - bh)
                xm = (jnp.where(is_t, qh, kh) * jnp.exp(arg)).astype(BF16)
                scores = jnp.where(lmat == lvl, _dot_nt(xm, xm), scores)
        oi_ref[0, :, sl] = _dot(scores.astype(BF16), v[:, sl])


def _mixer_in(has_prev, xs, modx, nw, win, lb, consts, wp, ps):
    tri, lvlm, pa, pinv = consts
    bsz, ntok, d = xs[0].shape
    nt = ntok // TILE
    tok = lambda w: pl.BlockSpec((1, TILE, w), lambda b, j: (b, j, 0))
    modspec = pl.BlockSpec((1, 1, 6, d), lambda b, j: (b, jnp.minimum(j, 1), 0, 0))
    full = lambda a: pl.BlockSpec(a.shape, lambda b, j: (0,) * a.ndim)
    sorted_rows = pl.BlockSpec((SORT_ROWS, d), lambda b, j: (b * nt + j, 0))
    in_specs = ([tok(d), sorted_rows, tok(128), modspec] if has_prev else [tok(d)]) + [
        modspec, full(nw), full(win), full(lb), full(tri), full(lvlm),
        pl.BlockSpec((1,) + pa.shape[1:], lambda b, j: (jnp.minimum(j, 1), 0, 0, 0)),
        pl.BlockSpec((1,) + pinv.shape[1:], lambda b, j: (jnp.minimum(j, 1), 0, 0)),
        full(wp), full(ps)]
    out_shape = [jax.ShapeDtypeStruct((bsz, ntok, HGRN_W), F32),
                 jax.ShapeDtypeStruct((bsz, ntok, 2 * HGRN_W), BF16),
                 jax.ShapeDtypeStruct((bsz, nt, 2, N_HEADS, HEAD_DIM, HEAD_DIM), F32),
                 jax.ShapeDtypeStruct((bsz, nt, 2, N_HEADS, 1, HEAD_DIM), F32),
                 jax.ShapeDtypeStruct((bsz, ntok, HGRN_W), BF16),
                 jax.ShapeDtypeStruct((bsz, ntok, POOL_W), BF16)]
    out_specs = [tok(HGRN_W), tok(2 * HGRN_W),
                 pl.BlockSpec((1, 1, 2, N_HEADS, HEAD_DIM, HEAD_DIM), lambda b, j: (b, j, 0, 0, 0, 0)),
                 pl.BlockSpec((1, 1, 2, N_HEADS, 1, HEAD_DIM), lambda b, j: (b, j, 0, 0, 0, 0)),
                 tok(HGRN_W), tok(POOL_W)]
    if has_prev:
        out_shape.append(jax.ShapeDtypeStruct((bsz, ntok, d), F32))
        out_specs.append(tok(d))
    args = list(xs) + [modx, nw, win, lb, tri, lvlm, pa, pinv, wp, ps]
    return pl.pallas_call(
        functools.partial(_mixer_in_kernel, has_prev),
        out_shape=out_shape, grid=(bsz, nt), in_specs=in_specs, out_specs=out_specs,
        compiler_params=pltpu.CompilerParams(dimension_semantics=("parallel", "parallel"),
                                             vmem_limit_bytes=VMEM_LIMIT),
        name="mixer_in",
    )(*args)


def _scan_kernel(utf_ref, af_ref, utb_ref, ab_ref, sf_ref, sb_ref, accf, accb):
    @pl.when(pl.program_id(1) == 0)
    def _():
        accf[...] = jnp.zeros_like(accf)
        accb[...] = jnp.zeros_like(accb)

    sf_ref[0, 0] = accf[...]
    sb_ref[0, 0] = accb[...]
    accf[...] = af_ref[0, 0, 0] * accf[...] + utf_ref[0, 0, 0]
    accb[...] = ab_ref[0, 0, 0] * accb[...] + utb_ref[0, 0, 0]


def _scan_state(ut, a):
    bsz, nt = ut.shape[:2]
    bwd = lambda s: jnp.where(s == 0, 0, nt - s)
    st = (N_HEADS, HEAD_DIM, HEAD_DIM)
    ut_spec = lambda d, o: pl.BlockSpec((1, 1, 1) + st, lambda b, s: (b, o(s), d, 0, 0, 0))
    a_spec = lambda d, o: pl.BlockSpec((1, 1, 1, N_HEADS, 1, HEAD_DIM), lambda b, s: (b, o(s), d, 0, 0, 0))
    ident = lambda s: s
    return pl.pallas_call(
        _scan_kernel,
        out_shape=[jax.ShapeDtypeStruct((bsz, nt) + st, F32)] * 2,
        grid=(bsz, nt),
        in_specs=[ut_spec(0, ident), a_spec(0, ident), ut_spec(1, bwd), a_spec(1, bwd)],
        out_specs=[pl.BlockSpec((1, 1) + st, lambda b, s: (b, s, 0, 0, 0)),
                   pl.BlockSpec((1, 1) + st, lambda b, s: (b, bwd(s), 0, 0, 0))],
        scratch_shapes=[pltpu.VMEM(st, F32), pltpu.VMEM(st, F32)],
        compiler_params=pltpu.CompilerParams(dimension_semantics=("parallel", "arbitrary"),
                                             vmem_limit_bytes=VMEM_LIMIT),
        name="scan_state",
    )(ut, a, ut, a)


def _route(scores, bias):
    t, e = scores.shape
    lane = lax.broadcasted_iota(jnp.int32, (t, e), 1)
    lane_f = lane.astype(F32)
    grp_f = (lane >> 3).astype(F32)
    big = float(4 * e)
    biased = scores + bias

    def first_max(x, idx_f):
        mx = jnp.max(x, axis=1, keepdims=True)
        return mx, jnp.min(jnp.where(x == mx, idx_f, big), axis=1, keepdims=True)

    grp_score = jnp.zeros_like(biased)
    for g in range(N_EXPERT_GROUPS):
        in_g = grp_f == float(g)
        xg = jnp.where(in_g, biased, NEG_INF)
        m1, i1 = first_max(xg, lane_f)
        m2 = jnp.max(jnp.where(lane_f == i1, NEG_INF, xg), axis=1, keepdims=True)
        grp_score = jnp.where(in_g, m1 + m2, grp_score)
    keep = jnp.zeros_like(biased)
    work = grp_score
    for _ in range(TOPK_GROUPS):
        _, gi = first_max(work, grp_f)
        hit = grp_f == gi
        keep = jnp.where(hit, 1.0, keep)
        work = jnp.where(hit, NEG_INF, work)
    work = jnp.where(keep > 0.0, biased, MASK_SCORE)
    sel = jnp.zeros_like(biased)
    hits = []
    for _ in range(TOP_K):
        _, ei = first_max(work, lane_f)
        hit = lane_f == ei
        hits.append(hit)
        sel = jnp.where(hit, 1.0, sel)
        work = jnp.where(hit, NEG_INF, work)
    gate = jnp.where(sel > 0.0, scores, 0.0)
    return gate / jnp.sum(gate, axis=1, keepdims=True) * ROUTED_SCALE, sel, hits


def _sort_rows(gates, sel, hits, h2b, tri_lower, xs_ref, posw_ref, cnt_ref):
    t, e = sel.shape
    incl = _dot(tri_lower, sel.astype(BF16))
    cnt = incl[t - 1:t, :]
    ngran = jnp.floor((cnt + float(GRAN - 1)) * (1.0 / GRAN))
    ei = lax.broadcasted_iota(jnp.int32, (e, e), 0)
    ej = lax.broadcasted_iota(jnp.int32, (e, e), 1)
    before = jnp.where(ei < ej, 1.0, 0.0).astype(BF16)
    run_start = _dot(jnp.broadcast_to(ngran, (8, e)).astype(BF16), before)[0:1, :] * float(GRAN)
    slot = run_start + incl - sel
    cnt_ref[0, 0] = jnp.broadcast_to(cnt, (8, e))

    lane = lax.broadcasted_iota(jnp.int32, (t, 128), 1)
    posw = jnp.where(lane < 8, -1.0, 0.0)
    for k, hit in enumerate(hits):
        pos_k = jnp.sum(jnp.where(hit, slot, 0.0), axis=1, keepdims=True)
        w_k = jnp.sum(jnp.where(hit, gates, 0.0), axis=1, keepdims=True)
        posw = jnp.where(lane == k, pos_k, jnp.where(lane == 8 + k, w_k, posw))
    posw_ref[0] = posw

    pos_t = posw.T
    row = lax.broadcasted_iota(jnp.int32, (TILE, t), 0).astype(F32)
    for rb in range(SORT_ROWS // TILE):
        onehot = jnp.zeros((TILE, t), F32)
        for k in range(TOP_K):
            onehot = onehot + jnp.where(row == pos_t[k:k + 1, :] - float(rb * TILE), 1.0, 0.0)
        xs_ref[rb * TILE:(rb + 1) * TILE, :] = _dot(onehot.astype(BF16), h2b).astype(BF16)


def _mixer_out_kernel(x_ref, oi_ref, qe_ref, sf_ref, sb_ref, og_ref, po_ref, mod_ref, gw_ref, wo_ref,
                      nw2_ref, wr_ref, rb_ref, sw1_ref, sw3_ref, sw2_ref, tri_ref,
                      xb_ref, xs_ref, posw_ref, cnt_ref):
    m = mod_ref[0, 0]
    parts = []
    for h in range(N_HEADS):
        sl = slice(h * HEAD_DIM, (h + 1) * HEAD_DIM)
        o = (oi_ref[0, :, sl]
             + _dot_nt(qe_ref[0, :, sl], sf_ref[0, 0, h].astype(BF16))
             + _dot_nt(qe_ref[0, :, HGRN_W + h * HEAD_DIM:HGRN_W + (h + 1) * HEAD_DIM],
                       sb_ref[0, 0, h].astype(BF16)))
        y = _rmsnorm(o, gw_ref[...]) * og_ref[0, :, sl].astype(F32)
        parts.append(y.astype(BF16))
    mix = jnp.concatenate(parts + [po_ref[0]], axis=-1)
    x1 = x_ref[0] + m[2:3] * _dot(mix, wo_ref[...])
    h2 = _rmsnorm(x1, nw2_ref[...]) * (1.0 + m[4:5]) + m[3:4]
    h2b = h2.astype(BF16)
    shared = _dot((_silu(_dot(h2b, sw1_ref[...])) * _dot(h2b, sw3_ref[...])).astype(BF16), sw2_ref[...])
    xb_ref[0] = x1 + m[5:6] * shared
    logits = jnp.dot(h2, wr_ref[...], preferred_element_type=F32, precision=lax.Precision.HIGHEST)
    gates, sel, hits = _route(_sigmoid(logits), rb_ref[...])
    _sort_rows(gates, sel, hits, h2b, tri_ref[0], xs_ref, posw_ref, cnt_ref)


def _mixer_out(x, oi, qe, sf, sb, og, po, modx, gw, wo, nw2, wr, rb, sw1, sw3, sw2, tri):
    bsz, ntok, d = x.shape
    nt = ntok // TILE
    tok = lambda w: pl.BlockSpec((1, TILE, w), lambda b, j: (b, j, 0))
    st = pl.BlockSpec((1, 1, N_HEADS, HEAD_DIM, HEAD_DIM), lambda b, j: (b, j, 0, 0, 0))
    full = lambda a: pl.BlockSpec(a.shape, lambda b, j: (0,) * a.ndim)
    return pl.pallas_call(
        _mixer_out_kernel,
        out_shape=[jax.ShapeDtypeStruct((bsz, ntok, d), F32),
                   jax.ShapeDtypeStruct((bsz * nt * SORT_ROWS, d), BF16),
                   jax.ShapeDtypeStruct((bsz, ntok, 128), F32),
                   jax.ShapeDtypeStruct((bsz, nt, 8, N_EXPERTS), F32)],
        grid=(bsz, nt),
        in_specs=[tok(d), tok(HGRN_W), tok(2 * HGRN_W), st, st, tok(HGRN_W), tok(POOL_W),
                  pl.BlockSpec((1, 1, 6, d), lambda b, j: (b, jnp.minimum(j, 1), 0, 0)),
                  full(gw), full(wo), full(nw2), full(wr), full(rb), full(sw1), full(sw3), full(sw2),
                  full(tri)],
        out_specs=[tok(d), pl.BlockSpec((SORT_ROWS, d), lambda b, j: (b * nt + j, 0)), tok(128),
                   pl.BlockSpec((1, 1, 8, N_EXPERTS), lambda b, j: (b, j, 0, 0))],
        compiler_params=pltpu.CompilerParams(dimension_semantics=("parallel", "parallel"),
                                             vmem_limit_bytes=VMEM_LIMIT),
        name="mixer_out",
    )(x, oi, qe, sf, sb, og, po, modx, gw, wo, nw2, wr, rb, sw1, sw3, sw2, tri)


def _moe_plan(cnt, nb_max):
    ntile, ne = cnt.shape
    ngran = (cnt + GRAN - 1) // GRAN
    run_start = jnp.cumsum(ngran, axis=1) - ngran
    ngran_t = ngran.T
    per_e = ngran_t.sum(axis=1)
    nblk = (per_e + GRAN_PER_BLOCK - 1) // GRAN_PER_BLOCK
    blk_end = jnp.cumsum(nblk)
    blk_start = blk_end - nblk
    nb = blk_end[-1]
    bidx = jnp.arange(nb_max, dtype=jnp.int32)
    be = jnp.sum((blk_end[None, :] <= jnp.minimum(bidx, nb - 1)[:, None]).astype(jnp.int32), axis=1)
    q0 = (bidx - blk_start[be]) * GRAN_PER_BLOCK
    nvalid = jnp.where(bidx < nb, jnp.clip(per_e[be] - q0, 0, GRAN_PER_BLOCK), 0)
    run_hi = jnp.cumsum(ngran_t, axis=1)
    run_lo = run_hi - ngran_t
    base = jnp.arange(ntile, dtype=jnp.int32)[None, :] * GRAN_PER_TILE + run_start.T - run_lo
    q = (q0[:, None] + jnp.arange(GRAN_PER_BLOCK, dtype=jnp.int32)[None, :])[:, :, None]
    in_run = (run_lo[be][:, None, :] <= q) & (q < run_hi[be][:, None, :])
    src = jnp.sum(jnp.where(in_run, base[be][:, None, :] + q, 0), axis=-1)
    return (nvalid.astype(jnp.int32), be, nb.reshape(1).astype(jnp.int32),
            src.reshape(-1).astype(jnp.int32))


def _moe_ffn_kernel(nv_ref, be_ref, nb_ref, src_ref, xs_hbm, w1_ref, w3_ref, w2_ref, ys_hbm,
                    xin, yout, sem_in, sem_out):
    b = pl.program_id(0)
    nb = nb_ref[0]
    slot = lax.rem(b, 2)

    def granule_copies(blk, sl, inbound, wait):
        nv = nv_ref[blk]
        for i in range(GRAN_PER_BLOCK):
            @pl.when(i < nv)
            def _():
                row = pl.multiple_of(src_ref[blk * GRAN_PER_BLOCK + i] * GRAN, GRAN)
                if inbound:
                    cp = pltpu.make_async_copy(xs_hbm.at[pl.ds(row, GRAN), :],
                                               xin.at[sl, pl.ds(i * GRAN, GRAN), :], sem_in.at[sl])
                else:
                    cp = pltpu.make_async_copy(yout.at[sl, pl.ds(i * GRAN, GRAN), :],
                                               ys_hbm.at[pl.ds(row, GRAN), :], sem_out.at[sl])
                if wait:
                    cp.wait()
                else:
                    cp.start()

    @pl.when(b < nb)
    def _():
        @pl.when(b == 0)
        def _():
            xin[...] = jnp.zeros_like(xin)
            granule_copies(b, slot, True, False)

        @pl.when(b + 1 < nb)
        def _():
            granule_copies(b + 1, 1 - slot, True, False)

        granule_copies(b, slot, True, True)

        @pl.when(b >= 2)
        def _():
            granule_copies(b - 2, slot, False, True)

        x = xin[slot]
        hid = (_silu(_dot(x, w1_ref[0])) * _dot(x, w3_ref[0])).astype(BF16)
        yout[slot] = _dot(hid, w2_ref[0]).astype(BF16)
        granule_copies(b, slot, False, False)

        @pl.when(b == nb - 1)
        def _():
            granule_copies(b, slot, False, True)

            @pl.when(b >= 1)
            def _():
                granule_copies(b - 1, 1 - slot, False, True)


def _moe_ffn(xs, plan, w1, w3, w2, nb_max):
    nrows, d = xs.shape
    ne, _, f = w1.shape
    rows = GRAN_PER_BLOCK * GRAN
    wspec = lambda shape: pl.BlockSpec((1,) + shape, lambda b, nv, be, nb, src: (be[b], 0, 0))
    return pl.pallas_call(
        _moe_ffn_kernel,
        out_shape=jax.ShapeDtypeStruct((nrows, d), BF16),
        grid_spec=pltpu.PrefetchScalarGridSpec(
            num_scalar_prefetch=4, grid=(nb_max,),
            in_specs=[pl.BlockSpec(memory_space=pl.ANY), wspec((d, f)), wspec((d, f)), wspec((f, d))],
            out_specs=pl.BlockSpec(memory_space=pl.ANY),
            scratch_shapes=[pltpu.VMEM((2, rows, d), BF16), pltpu.VMEM((2, rows, d), BF16),
                            pltpu.SemaphoreType.DMA((2,)), pltpu.SemaphoreType.DMA((2,))]),
        input_output_aliases={4: 0},
        compiler_params=pltpu.CompilerParams(dimension_semantics=("arbitrary",),
                                             vmem_limit_bytes=VMEM_LIMIT),
        name="moe_ffn",
    )(*plan, xs, w1, w3, w2)


def _final_kernel(xb_ref, ys_ref, posw_ref, mod_ref, nw_ref, o_ref):
    x = xb_ref[0] + mod_ref[0, 0][5:6] * _combine(ys_ref, posw_ref[0])
    o_ref[0] = _rmsnorm(x, nw_ref[...])


def _final(xb, ys, posw, modx, nw, n_ctx_tiles):
    bsz, ntok, d = xb.shape
    nt_all = ntok // TILE
    nt = nt_all - n_ctx_tiles
    lat = lambda w: pl.BlockSpec((1, TILE, w), lambda b, j: (b, j + n_ctx_tiles, 0))
    return pl.pallas_call(
        _final_kernel,
        out_shape=jax.ShapeDtypeStruct((bsz, nt * TILE, d), F32),
        grid=(bsz, nt),
        in_specs=[lat(d), pl.BlockSpec((SORT_ROWS, d), lambda b, j: (b * nt_all + j + n_ctx_tiles, 0)),
                  lat(128), pl.BlockSpec((1, 1, 6, d), lambda b, j: (b, 1, 0, 0)),
                  pl.BlockSpec(nw.shape, lambda b, j: (0, 0))],
        out_specs=pl.BlockSpec((1, TILE, d), lambda b, j: (b, j, 0)),
        compiler_params=pltpu.CompilerParams(dimension_semantics=("parallel", "parallel"),
                                             vmem_limit_bytes=VMEM_LIMIT),
        name="final_norm",
    )(xb, ys, posw, modx, nw)


def kernel(x, c, ctx, c_ctx, norm1_w, norm2_w, w_ada, b_ada, w_in, hgrn_lb_logits, hgrn_gnorm_w, w_pool,
           pool_scale, w_out, w_router, router_bias, w1_experts, w3_experts, w2_experts, w1_shared,
           w3_shared, w2_shared, final_norm_w):
    bsz, seq_len, d = x.shape
    ctx_len = ctx.shape[1]
    depth = w_ada.shape[0]
    assert ctx_len == TILE and seq_len % TILE == 0 and seq_len % GRID_W == 0
    ntok = ctx_len + seq_len
    ntile = bsz * ntok // TILE
    nb_max = -(-ntile * MAX_GRAN_PER_TILE // GRAN_PER_BLOCK) + N_EXPERTS

    lf, lbm = _level_matrices()
    lower, upper = _tri_matrices()
    pa, pinv = _pool_matrices()
    consts = (jnp.asarray(np.stack([lower, upper]), BF16), jnp.asarray(np.stack([lf, lbm])),
              jnp.asarray(pa, BF16), jnp.asarray(pinv))

    p_lb = jax.nn.softmax(hgrn_lb_logits.astype(F32), axis=0)
    lb_all = jnp.cumsum(p_lb, axis=0) - p_lb[0:1]

    rows = 16
    cvec = jnp.concatenate([c, c_ctx[None, :], jnp.zeros((rows - bsz - 1, d), F32)], axis=0)
    mod = _ada_table(cvec, w_ada, b_ada)
    mod_lat = mod[:, :bsz].reshape(depth, bsz, 6, d)
    mod_ctx = jnp.broadcast_to(mod[:, bsz].reshape(depth, 1, 6, d), (depth, bsz, 6, d))
    modx = jnp.stack([mod_ctx, mod_lat], axis=2)

    xs = (jnp.concatenate([ctx, x], axis=1),)
    for l in range(depth):
        has_prev = l > 0
        outs = _mixer_in(has_prev, xs + ((modx[l - 1],) if has_prev else ()), modx[l], norm1_w[l][None, :],
                         w_in[l].astype(BF16), lb_all[l], consts, w_pool[l].astype(BF16),
                         pool_scale[l][None, :])
        oi, qe, ut, a, og, po = outs[:6]
        x_cur = outs[6] if has_prev else xs[0]
        sf, sb = _scan_state(ut, a)
        xb, rows_sorted, posw, cnt = _mixer_out(
            x_cur, oi, qe, sf, sb, og, po, modx[l], hgrn_gnorm_w[l][None, :], w_out[l].astype(BF16),
            norm2_w[l][None, :], w_router[l], router_bias[l][None, :], w1_shared[l].astype(BF16),
            w3_shared[l].astype(BF16), w2_shared[l].astype(BF16), consts[0])
        plan = _moe_plan(cnt[:, :, 0, :].reshape(ntile, N_EXPERTS).astype(jnp.int32), nb_max)
        ys = _moe_ffn(rows_sorted, plan, w1_experts[l].astype(BF16), w3_experts[l].astype(BF16),
                      w2_experts[l].astype(BF16), nb_max)
        xs = (xb, ys, posw)
    return _final(xs[0], xs[1], xs[2], modx[depth - 1], final_norm_w[None, :], ctx_len // TILE)
```

```python
import functools

import numpy as np
import jax
import jax.numpy as jnp
from jax import lax
from jax.experimental import pallas as pl
from jax.experimental.pallas import tpu as pltpu

F32 = jnp.float32
BF16 = jnp.bfloat16

EPS = 1e-6
LB_FLOOR = 1e-30
MASK_SCORE = -1e4
GRID_W = 64
HEAD_DIM = 128
N_HEADS = 4
HGRN_W = N_HEADS * HEAD_DIM
POOL_WINDOWS = (2, 4, 8, 16)
POOL_GROUPS = len(POOL_WINDOWS)
POOL_GROUP_DIM = 128
POOL_W = POOL_GROUPS * POOL_GROUP_DIM
N_EXPERTS = 64
TOP_K = 6
N_EXPERT_GROUPS = 8
EXPERTS_PER_GROUP = N_EXPERTS // N_EXPERT_GROUPS
TOPK_GROUPS = 4
ROUTED_SCALE = 2.5

TILE = 256
N_LEVELS = 8
GRAN = 16
MAX_GRAN_PER_TILE = (TILE * TOP_K + N_EXPERTS * (GRAN - 1)) // GRAN
GRAN_PER_TILE = -(-MAX_GRAN_PER_TILE * GRAN // TILE) * TILE // GRAN
SORT_ROWS = GRAN_PER_TILE * GRAN
GRAN_PER_BLOCK = 32
VMEM_LIMIT = 56 * 1024 * 1024
NEG_INF = float("-inf")
LOG2E = 1.4426950408889634


def _sigmoid(z):
    return 1.0 / (1.0 + jnp.exp(-z))


def _silu(z):
    return z * _sigmoid(z)


def _dot(a, b):
    return jnp.dot(a, b, preferred_element_type=F32)


def _dot_nt(a, b):
    return lax.dot_general(a, b, (((1,), (1,)), ((), ())), preferred_element_type=F32)


def _dot_tn(a, b):
    return lax.dot_general(a, b, (((0,), (0,)), ((), ())), preferred_element_type=F32)


def _split3(g):
    hi = g.astype(BF16)
    r1 = g - hi.astype(F32)
    mid = r1.astype(BF16)
    lo = (r1 - mid.astype(F32)).astype(BF16)
    return hi, mid, lo


def _rmsnorm(x, w):
    return x * lax.rsqrt(jnp.mean(x * x, axis=-1, keepdims=True) + EPS) * w


def _level_matrices():
    t = np.arange(TILE)
    xor = t[:, None] ^ t[None, :]
    lvl = np.floor(np.log2(np.maximum(xor, 1))).astype(np.int32)
    lf = np.where(t[:, None] > t[None, :], lvl, -1).astype(np.int32)
    return lf, lf.T.copy()


def _tri_matrices():
    t = np.arange(TILE)
    lower = (t[:, None] >= t[None, :]).astype(np.float32)
    return lower, lower.T.copy()


def _pool_matrices():
    a = np.zeros((2, POOL_GROUPS, TILE, TILE), np.float32)
    inv = np.zeros((2, TILE, POOL_W), np.float32)
    t = np.arange(TILE)
    for kind, n in ((0, TILE), (1, GRID_W)):
        pos, base = t % n, (t // n) * n
        for gi, w in enumerate(POOL_WINDOWS):
            lo = np.clip(pos - w // 2, 0, n)
            hi = np.clip(pos + w - w // 2, 0, n)
            s = t[None, :]
            a[kind, gi] = ((s >= (base + lo)[:, None]) & (s < (base + hi)[:, None])).astype(np.float32)
            inv[kind, :, gi * POOL_GROUP_DIM:(gi + 1) * POOL_GROUP_DIM] = (1.0 / (hi - lo))[:, None]
    return a, inv


def _ada_kernel(c_ref, w_ref, b_ref, o_ref):
    o_ref[0] = jnp.dot(_silu(c_ref[...]), w_ref[0], preferred_element_type=F32,
                       precision=lax.Precision.HIGHEST) + b_ref[0]


def _ada_table(cvec, w_ada, b_ada):
    depth, d, n = w_ada.shape
    nb = 4
    rows = cvec.shape[0]
    return pl.pallas_call(
        _ada_kernel,
        out_shape=jax.ShapeDtypeStruct((depth, rows, n), F32),
        grid=(depth, nb),
        in_specs=[pl.BlockSpec((rows, d), lambda l, j: (0, 0)),
                  pl.BlockSpec((1, d, n // nb), lambda l, j: (l, 0, j)),
                  pl.BlockSpec((1, 1, n // nb), lambda l, j: (l, 0, j))],
        out_specs=pl.BlockSpec((1, rows, n // nb), lambda l, j: (l, 0, j)),
        compiler_params=pltpu.CompilerParams(dimension_semantics=("arbitrary", "arbitrary"),
                                             vmem_limit_bytes=VMEM_LIMIT),
        name="ada_table",
    )(cvec, w_ada, b_ada.reshape(depth, 1, n))


def _reference_rows(b, lvl, backward):
    w = 1 << lvl
    pm = w if backward else w - 1
    if w >= 4:
        n = TILE // (2 * w)
        row = b.reshape(n, 2 * w, HEAD_DIM)[:, pm:pm + 1, :]
        return jnp.broadcast_to(row, (n, 2 * w, HEAD_DIM)).reshape(TILE, HEAD_DIM)
    pos = lax.broadcasted_iota(jnp.int32, (TILE, HEAD_DIM), 0) & (2 * w - 1)
    out = b
    for p in range(2 * w):
        if p == pm:
            continue
        out = jnp.where(pos == p, pltpu.roll(b, (p - pm) % TILE, axis=0), out)
    return out


def _block_positions(posw, rb):
    rel = posw[0:8, :] - float(rb * TILE)
    rel = jnp.where((rel >= 0.0) & (rel < float(TILE)), rel, -1.0).astype(BF16)
    return [rel[k:k + 1, :] for k in range(TOP_K)]


def _combine(ys_ref, posw):
    row = lax.broadcasted_iota(jnp.int32, (TILE, TILE), 0).astype(F32).astype(BF16)
    wts = posw[8:16, :].astype(BF16)
    zero = jnp.zeros((TILE, TILE), BF16)
    acc = None
    for rb in range(SORT_ROWS // TILE):
        pw = zero
        for k, rel in enumerate(_block_positions(posw, rb)):
            pw = pw + jnp.where(row == rel, wts[k:k + 1, :], zero)
        part = _dot_tn(pw, ys_ref[rb * TILE:(rb + 1) * TILE, :])
        acc = part if acc is None else acc + part
    return acc


def _mixer_in_kernel(has_prev, *refs):
    if has_prev:
        xb_ref, ys_ref, posw_ref, modp_ref = refs[:4]
        refs = refs[4:]
    else:
        x_ref = refs[0]
        refs = refs[1:]
    (mod_ref, nw_ref, win_ref, lb_ref, tri_ref, lvl_ref, pa_ref, pinv_ref, wp_ref, ps_ref) = refs[:10]
    outs = refs[10:]
    if has_prev:
        oi_ref, qe_ref, ut_ref, a_ref, og_ref, po_ref, xo_ref = outs
        x = xb_ref[0] + modp_ref[0, 0][5:6] * _combine(ys_ref, posw_ref[0, 0])
        xo_ref[0] = x
    else:
        oi_ref, qe_ref, ut_ref, a_ref, og_ref, po_ref = outs
        x = x_ref[0]

    m = mod_ref[0, 0]
    hx = (_rmsnorm(x, nw_ref[...]) * (1.0 + m[1:2]) + m[0:1]).astype(BF16)

    def proj(col):
        return _dot(hx, win_ref[:, col * HGRN_W:(col + 1) * HGRN_W])

    v = proj(2).astype(BF16)
    q = _silu(proj(3)) * (HEAD_DIM ** -0.5)
    og_ref[0] = _silu(proj(4)).astype(BF16)

    p = proj(5)
    pb = p.astype(BF16)
    pinv = pinv_ref[0]
    for g in range(POOL_GROUPS):
        sl = slice(g * POOL_GROUP_DIM, (g + 1) * POOL_GROUP_DIM)
        pooled = _dot(pa_ref[0, g], pb[:, sl]) * pinv[:, sl] - p[:, sl]
        po_ref[0, :, sl] = (_dot(pooled.astype(BF16), wp_ref[g]) * ps_ref[:, sl]).astype(BF16)

    ks, bs = [], []
    for d in range(2):
        z = proj(d)
        lb = lb_ref[d:d + 1, :]
        sg = _sigmoid(z)
        kk = (1.0 - lb) * (1.0 - sg)
        g = jnp.log(jnp.maximum(lb, LB_FLOOR) + (1.0 - lb) * sg)
        hi, mid, lo = _split3(g)
        tri = tri_ref[d]
        b = _dot(tri, hi) + _dot(tri, mid) + _dot(tri, lo)
        b_tot = b[TILE - 1:TILE, :] if d == 0 else b[0:1, :]
        qe_ref[0, :, d * HGRN_W:(d + 1) * HGRN_W] = (q * jnp.exp(b)).astype(BF16)
        k_end = (kk * jnp.exp(b_tot - b)).astype(BF16)
        a_tot = jnp.exp(b_tot)
        for h in range(N_HEADS):
            sl = slice(h * HEAD_DIM, (h + 1) * HEAD_DIM)
            ut_ref[0, 0, d, h] = _dot_tn(v[:, sl], k_end[:, sl])
            a_ref[0, 0, d, h] = a_tot[:, sl]
        ks.append(kk)
        bs.append(b)

    rows = lax.broadcasted_iota(jnp.int32, (TILE, HEAD_DIM), 0)
    upper_f32 = [((rows >> lvl) & 1) == 1 for lvl in range(N_LEVELS)]
    upper_b16 = [jnp.where(u, 1.0, 0.0).astype(BF16) > 0 for u in upper_f32]
    eye = (lax.broadcasted_iota(jnp.int32, (TILE, TILE), 0)
           == lax.broadcasted_iota(jnp.int32, (TILE, TILE), 1))
    lmat = lvl_ref[...]
    zero = jnp.zeros((TILE, HEAD_DIM), BF16)
    for h in range(N_HEADS):
        sl = slice(h * HEAD_DIM, (h + 1) * HEAD_DIM)
        qh = q[:, sl]
        qb, kf, kb = qh.astype(BF16), ks[0][:, sl].astype(BF16), ks[1][:, sl].astype(BF16)
        bf, bb = bs[0][:, sl] * LOG2E, bs[1][:, sl] * LOG2E
        diag = jnp.sum(qh * (ks[0][:, sl] + ks[1][:, sl]), axis=-1, keepdims=True)
        scores = jnp.where(eye, diag, 0.0)
        for lvl in range(N_LEVELS):
            up, upb = upper_f32[lvl], upper_b16[lvl]
            ref_f = _reference_rows(bf, lvl, False)
            ref_b = _reference_rows(bb, lvl, True)
            arg_f = jnp.where(up, bf - ref_f, ref_f - bf)
            arg_b = jnp.where(up, ref_b - bb, bb - ref_b)
            xf = jnp.where(upb, qb, kf) * jnp.exp2(arg_f.astype(BF16))
            xb = jnp.where(upb, kb, qb) * jnp.exp2(arg_b.astype(BF16))
            lhs = jnp.concatenate([jnp.where(upb, xf, zero), jnp.where(upb, zero, xb)], axis=-1)
            rhs = jnp.concatenate([xf, xb], axis=-1)
            scores = jnp.where(lmat == lvl, _dot_nt(lhs, rhs), scores)
        oi_ref[0, :, sl] = _dot(scores.astype(BF16), v[:, sl])


def _mixer_in(has_prev, xs, modx, nw, win, lb, consts, wp, ps):
    tri, lvlm, pa, pinv = consts
    bsz, ntok, d = xs[0].shape
    nt = ntok // TILE
    tok = lambda w: pl.BlockSpec((1, TILE, w), lambda b, j: (b, j, 0))
    modspec = pl.BlockSpec((1, 1, 6, d), lambda b, j: (b, jnp.minimum(j, 1), 0, 0))
    full = lambda a: pl.BlockSpec(a.shape, lambda b, j: (0,) * a.ndim)
    sorted_rows = pl.BlockSpec((SORT_ROWS, d), lambda b, j: (b * nt + j, 0))
    picks = pl.BlockSpec((1, 1, 16, TILE), lambda b, j: (b, j, 0, 0))
    in_specs = ([tok(d), sorted_rows, picks, modspec] if has_prev else [tok(d)]) + [
        modspec, full(nw), full(win), full(lb), full(tri), full(lvlm),
        pl.BlockSpec((1,) + pa.shape[1:], lambda b, j: (jnp.minimum(j, 1), 0, 0, 0)),
        pl.BlockSpec((1,) + pinv.shape[1:], lambda b, j: (jnp.minimum(j, 1), 0, 0)),
        full(wp), full(ps)]
    out_shape = [jax.ShapeDtypeStruct((bsz, ntok, HGRN_W), F32),
                 jax.ShapeDtypeStruct((bsz, ntok, 2 * HGRN_W), BF16),
                 jax.ShapeDtypeStruct((bsz, nt, 2, N_HEADS, HEAD_DIM, HEAD_DIM), F32),
                 jax.ShapeDtypeStruct((bsz, nt, 2, N_HEADS, 1, HEAD_DIM), F32),
                 jax.ShapeDtypeStruct((bsz, ntok, HGRN_W), BF16),
                 jax.ShapeDtypeStruct((bsz, ntok, POOL_W), BF16)]
    out_specs = [tok(HGRN_W), tok(2 * HGRN_W),
                 pl.BlockSpec((1, 1, 2, N_HEADS, HEAD_DIM, HEAD_DIM), lambda b, j: (b, j, 0, 0, 0, 0)),
                 pl.BlockSpec((1, 1, 2, N_HEADS, 1, HEAD_DIM), lambda b, j: (b, j, 0, 0, 0, 0)),
                 tok(HGRN_W), tok(POOL_W)]
    if has_prev:
        out_shape.append(jax.ShapeDtypeStruct((bsz, ntok, d), F32))
        out_specs.append(tok(d))
    args = list(xs) + [modx, nw, win, lb, tri, lvlm, pa, pinv, wp, ps]
    return pl.pallas_call(
        functools.partial(_mixer_in_kernel, has_prev),
        out_shape=out_shape, grid=(bsz, nt), in_specs=in_specs, out_specs=out_specs,
        compiler_params=pltpu.CompilerParams(dimension_semantics=("parallel", "parallel"),
                                             vmem_limit_bytes=VMEM_LIMIT),
        name="mixer_in",
    )(*args)


def _scan_kernel(utf_ref, af_ref, utb_ref, ab_ref, sf_ref, sb_ref, accf, accb):
    @pl.when(pl.program_id(1) == 0)
    def _():
        accf[...] = jnp.zeros_like(accf)
        accb[...] = jnp.zeros_like(accb)

    sf_ref[0, 0] = accf[...]
    sb_ref[0, 0] = accb[...]
    accf[...] = af_ref[0, 0, 0] * accf[...] + utf_ref[0, 0, 0]
    accb[...] = ab_ref[0, 0, 0] * accb[...] + utb_ref[0, 0, 0]


def _scan_state(ut, a):
    bsz, nt = ut.shape[:2]
    bwd = lambda s: jnp.where(s == 0, 0, nt - s)
    st = (N_HEADS, HEAD_DIM, HEAD_DIM)
    ut_spec = lambda d, o: pl.BlockSpec((1, 1, 1) + st, lambda b, s: (b, o(s), d, 0, 0, 0))
    a_spec = lambda d, o: pl.BlockSpec((1, 1, 1, N_HEADS, 1, HEAD_DIM), lambda b, s: (b, o(s), d, 0, 0, 0))
    ident = lambda s: s
    return pl.pallas_call(
        _scan_kernel,
        out_shape=[jax.ShapeDtypeStruct((bsz, nt) + st, F32)] * 2,
        grid=(bsz, nt),
        in_specs=[ut_spec(0, ident), a_spec(0, ident), ut_spec(1, bwd), a_spec(1, bwd)],
        out_specs=[pl.BlockSpec((1, 1) + st, lambda b, s: (b, s, 0, 0, 0)),
                   pl.BlockSpec((1, 1) + st, lambda b, s: (b, bwd(s), 0, 0, 0))],
        scratch_shapes=[pltpu.VMEM(st, F32), pltpu.VMEM(st, F32)],
        compiler_params=pltpu.CompilerParams(dimension_semantics=("parallel", "arbitrary"),
                                             vmem_limit_bytes=VMEM_LIMIT),
        name="scan_state",
    )(ut, a, ut, a)


def _route(scores, bias):
    e, t = scores.shape
    ng, per = N_EXPERT_GROUPS, EXPERTS_PER_GROUP
    biased = scores + bias
    grouped = biased.reshape(ng, per, t)
    member = lax.broadcasted_iota(jnp.int32, (ng, per, t), 1).astype(F32)
    m1 = jnp.max(grouped, axis=1, keepdims=True)
    i1 = jnp.min(jnp.where(grouped == m1, member, float(per)), axis=1, keepdims=True)
    m2 = jnp.max(jnp.where(member == i1, NEG_INF, grouped), axis=1, keepdims=True)
    work = jnp.broadcast_to(m1 + m2, (ng, per, t)).reshape(e, t)
    eidx = lax.broadcasted_iota(jnp.int32, (e, t), 0)
    gidx = (eidx >> 3).astype(F32)
    eidx = eidx.astype(F32)

    def first_max(x, idx):
        mx = jnp.max(x, axis=0, keepdims=True)
        return jnp.min(jnp.where(x == mx, idx, float(4 * e)), axis=0, keepdims=True)

    keep = jnp.zeros_like(biased)
    for _ in range(TOPK_GROUPS):
        hit = gidx == first_max(work, gidx)
        keep = jnp.where(hit, 1.0, keep)
        work = jnp.where(hit, NEG_INF, work)
    work = jnp.where(keep > 0.0, biased, MASK_SCORE)
    sel = jnp.zeros_like(biased)
    hits = []
    for _ in range(TOP_K):
        hit = eidx == first_max(work, eidx)
        hits.append(hit)
        sel = jnp.where(hit, 1.0, sel)
        work = jnp.where(hit, NEG_INF, work)
    gate = jnp.where(sel > 0.0, scores, 0.0)
    return gate / jnp.sum(gate, axis=0, keepdims=True) * ROUTED_SCALE, sel, hits


def _sort_rows(gates, sel, hits, h2b, tri_upper, xs_ref, posw_ref, cnt_ref):
    e, t = sel.shape
    selb = sel.astype(BF16)
    incl = _dot(selb, tri_upper)
    cnt = _dot(selb, jnp.ones((t, 128), BF16))
    ngran = jnp.floor((cnt + float(GRAN - 1)) * (1.0 / GRAN))
    ei = lax.broadcasted_iota(jnp.int32, (e, e), 0)
    ej = lax.broadcasted_iota(jnp.int32, (e, e), 1)
    earlier = jnp.where(ej < ei, 1.0, 0.0).astype(BF16)
    run_start = _dot(earlier, ngran.astype(BF16)) * float(GRAN)
    slot = run_start[:, 0:1] + incl - sel
    cnt_ref[0, 0] = cnt

    ridx = lax.broadcasted_iota(jnp.int32, (16, t), 0)
    posw = jnp.where(ridx < 8, -1.0, 0.0)
    for k, hit in enumerate(hits):
        pos_k = jnp.sum(jnp.where(hit, slot, 0.0), axis=0, keepdims=True)
        w_k = jnp.sum(jnp.where(hit, gates, 0.0), axis=0, keepdims=True)
        posw = jnp.where(ridx == k, pos_k, jnp.where(ridx == 8 + k, w_k, posw))
    posw_ref[0, 0] = posw

    row = lax.broadcasted_iota(jnp.int32, (TILE, t), 0).astype(F32).astype(BF16)
    one, zero = jnp.ones((TILE, t), BF16), jnp.zeros((TILE, t), BF16)
    for rb in range(SORT_ROWS // TILE):
        onehot = zero
        for rel in _block_positions(posw, rb):
            onehot = onehot + jnp.where(row == rel, one, zero)
        xs_ref[rb * TILE:(rb + 1) * TILE, :] = _dot(onehot, h2b).astype(BF16)


def _mixer_out_kernel(x_ref, oi_ref, qe_ref, sf_ref, sb_ref, og_ref, po_ref, mod_ref, gw_ref, wo_ref,
                      nw2_ref, wr_ref, rb_ref, sw1_ref, sw3_ref, sw2_ref, tri_ref,
                      xb_ref, xs_ref, posw_ref, cnt_ref):
    m = mod_ref[0, 0]
    parts = []
    for h in range(N_HEADS):
        sl = slice(h * HEAD_DIM, (h + 1) * HEAD_DIM)
        o = (oi_ref[0, :, sl]
             + _dot_nt(qe_ref[0, :, sl], sf_ref[0, 0, h].astype(BF16))
             + _dot_nt(qe_ref[0, :, HGRN_W + h * HEAD_DIM:HGRN_W + (h + 1) * HEAD_DIM],
                       sb_ref[0, 0, h].astype(BF16)))
        y = _rmsnorm(o, gw_ref[...]) * og_ref[0, :, sl].astype(F32)
        parts.append(y.astype(BF16))
    mix = jnp.concatenate(parts + [po_ref[0]], axis=-1)
    x1 = x_ref[0] + m[2:3] * _dot(mix, wo_ref[...])
    h2 = _rmsnorm(x1, nw2_ref[...]) * (1.0 + m[4:5]) + m[3:4]
    h2b = h2.astype(BF16)
    shared = _dot((_silu(_dot(h2b, sw1_ref[...])) * _dot(h2b, sw3_ref[...])).astype(BF16), sw2_ref[...])
    xb_ref[0] = x1 + m[5:6] * shared
    h2lo = (h2 - h2b.astype(F32)).astype(BF16)
    logits = _dot_nt(wr_ref[0], h2b) + _dot_nt(wr_ref[0], h2lo) + _dot_nt(wr_ref[1], h2b)
    gates, sel, hits = _route(_sigmoid(logits), rb_ref[...])
    _sort_rows(gates, sel, hits, h2b, tri_ref[1], xs_ref, posw_ref, cnt_ref)


def _mixer_out(x, oi, qe, sf, sb, og, po, modx, gw, wo, nw2, wr, rb, sw1, sw3, sw2, tri):
    bsz, ntok, d = x.shape
    nt = ntok // TILE
    tok = lambda w: pl.BlockSpec((1, TILE, w), lambda b, j: (b, j, 0))
    st = pl.BlockSpec((1, 1, N_HEADS, HEAD_DIM, HEAD_DIM), lambda b, j: (b, j, 0, 0, 0))
    full = lambda a: pl.BlockSpec(a.shape, lambda b, j: (0,) * a.ndim)
    return pl.pallas_call(
        _mixer_out_kernel,
        out_shape=[jax.ShapeDtypeStruct((bsz, ntok, d), F32),
                   jax.ShapeDtypeStruct((bsz * nt * SORT_ROWS, d), BF16),
                   jax.ShapeDtypeStruct((bsz, nt, 16, TILE), F32),
                   jax.ShapeDtypeStruct((bsz, nt, N_EXPERTS, 128), F32)],
        grid=(bsz, nt),
        in_specs=[tok(d), tok(HGRN_W), tok(2 * HGRN_W), st, st, tok(HGRN_W), tok(POOL_W),
                  pl.BlockSpec((1, 1, 6, d), lambda b, j: (b, jnp.minimum(j, 1), 0, 0)),
                  full(gw), full(wo), full(nw2), full(wr), full(rb), full(sw1), full(sw3), full(sw2),
                  full(tri)],
        out_specs=[tok(d), pl.BlockSpec((SORT_ROWS, d), lambda b, j: (b * nt + j, 0)),
                   pl.BlockSpec((1, 1, 16, TILE), lambda b, j: (b, j, 0, 0)),
                   pl.BlockSpec((1, 1, N_EXPERTS, 128), lambda b, j: (b, j, 0, 0))],
        compiler_params=pltpu.CompilerParams(dimension_semantics=("parallel", "parallel"),
                                             vmem_limit_bytes=VMEM_LIMIT),
        name="mixer_out",
    )(x, oi, qe, sf, sb, og, po, modx, gw, wo, nw2, wr, rb, sw1, sw3, sw2, tri)


def _moe_plan(cnt, nb_max):
    ntile, ne = cnt.shape
    ngran = (cnt + GRAN - 1) // GRAN
    run_start = jnp.cumsum(ngran, axis=1) - ngran
    ngran_t = ngran.T
    per_e = ngran_t.sum(axis=1)
    nblk = (per_e + GRAN_PER_BLOCK - 1) // GRAN_PER_BLOCK
    blk_end = jnp.cumsum(nblk)
    blk_start = blk_end - nblk
    nb = blk_end[-1]
    bidx = jnp.arange(nb_max, dtype=jnp.int32)
    be = jnp.sum((blk_end[None, :] <= jnp.minimum(bidx, nb - 1)[:, None]).astype(jnp.int32), axis=1)
    q0 = (bidx - blk_start[be]) * GRAN_PER_BLOCK
    nvalid = jnp.where(bidx < nb, jnp.clip(per_e[be] - q0, 0, GRAN_PER_BLOCK), 0)
    run_hi = jnp.cumsum(ngran_t, axis=1)
    run_lo = run_hi - ngran_t
    base = jnp.arange(ntile, dtype=jnp.int32)[None, :] * GRAN_PER_TILE + run_start.T - run_lo
    q = (q0[:, None] + jnp.arange(GRAN_PER_BLOCK, dtype=jnp.int32)[None, :])[:, :, None]
    in_run = (run_lo[be][:, None, :] <= q) & (q < run_hi[be][:, None, :])
    src = jnp.sum(jnp.where(in_run, base[be][:, None, :] + q, 0), axis=-1)
    return (nvalid.astype(jnp.int32), be, nb.reshape(1).astype(jnp.int32),
            src.reshape(-1).astype(jnp.int32))


def _moe_ffn_kernel(nv_ref, be_ref, nb_ref, src_ref, xs_hbm, w1_ref, w3_ref, w2_ref, ys_hbm,
                    xin, yout, sem_in, sem_out):
    b = pl.program_id(0)
    nb = nb_ref[0]
    slot = lax.rem(b, 2)

    def granule_copies(blk, sl, inbound, wait):
        nv = nv_ref[blk]
        for i in range(GRAN_PER_BLOCK):
            @pl.when(i < nv)
            def _():
                row = pl.multiple_of(src_ref[blk * GRAN_PER_BLOCK + i] * GRAN, GRAN)
                if inbound:
                    cp = pltpu.make_async_copy(xs_hbm.at[pl.ds(row, GRAN), :],
                                               xin.at[sl, pl.ds(i * GRAN, GRAN), :], sem_in.at[sl])
                else:
                    cp = pltpu.make_async_copy(yout.at[sl, pl.ds(i * GRAN, GRAN), :],
                                               ys_hbm.at[pl.ds(row, GRAN), :], sem_out.at[sl])
                if wait:
                    cp.wait()
                else:
                    cp.start()

    @pl.when(b < nb)
    def _():
        @pl.when(b == 0)
        def _():
            xin[...] = jnp.zeros_like(xin)
            granule_copies(b, slot, True, False)

        @pl.when(b + 1 < nb)
        def _():
            granule_copies(b + 1, 1 - slot, True, False)

        granule_copies(b, slot, True, True)

        @pl.when(b >= 2)
        def _():
            granule_copies(b - 2, slot, False, True)

        x = xin[slot]
        hid = (_silu(_dot(x, w1_ref[0])) * _dot(x, w3_ref[0])).astype(BF16)
        yout[slot] = _dot(hid, w2_ref[0]).astype(BF16)
        granule_copies(b, slot, False, False)

        @pl.when(b == nb - 1)
        def _():
            granule_copies(b, slot, False, True)

            @pl.when(b >= 1)
            def _():
                granule_copies(b - 1, 1 - slot, False, True)


def _moe_ffn(xs, plan, w1, w3, w2, nb_max):
    nrows, d = xs.shape
    ne, _, f = w1.shape
    rows = GRAN_PER_BLOCK * GRAN
    wspec = lambda shape: pl.BlockSpec((1,) + shape, lambda b, nv, be, nb, src: (be[b], 0, 0))
    return pl.pallas_call(
        _moe_ffn_kernel,
        out_shape=jax.ShapeDtypeStruct((nrows, d), BF16),
        grid_spec=pltpu.PrefetchScalarGridSpec(
            num_scalar_prefetch=4, grid=(nb_max,),
            in_specs=[pl.BlockSpec(memory_space=pl.ANY), wspec((d, f)), wspec((d, f)), wspec((f, d))],
            out_specs=pl.BlockSpec(memory_space=pl.ANY),
            scratch_shapes=[pltpu.VMEM((2, rows, d), BF16), pltpu.VMEM((2, rows, d), BF16),
                            pltpu.SemaphoreType.DMA((2,)), pltpu.SemaphoreType.DMA((2,))]),
        input_output_aliases={4: 0},
        compiler_params=pltpu.CompilerParams(dimension_semantics=("arbitrary",),
                                             vmem_limit_bytes=VMEM_LIMIT),
        name="moe_ffn",
    )(*plan, xs, w1, w3, w2)


def _final_kernel(xb_ref, ys_ref, posw_ref, mod_ref, nw_ref, o_ref):
    x = xb_ref[0] + mod_ref[0, 0][5:6] * _combine(ys_ref, posw_ref[0, 0])
    o_ref[0] = _rmsnorm(x, nw_ref[...])


def _final(xb, ys, posw, modx, nw, n_ctx_tiles):
    bsz, ntok, d = xb.shape
    nt_all = ntok // TILE
    nt = nt_all - n_ctx_tiles
    lat = lambda w: pl.BlockSpec((1, TILE, w), lambda b, j: (b, j + n_ctx_tiles, 0))
    return pl.pallas_call(
        _final_kernel,
        out_shape=jax.ShapeDtypeStruct((bsz, nt * TILE, d), F32),
        grid=(bsz, nt),
        in_specs=[lat(d), pl.BlockSpec((SORT_ROWS, d), lambda b, j: (b * nt_all + j + n_ctx_tiles, 0)),
                  pl.BlockSpec((1, 1, 16, TILE), lambda b, j: (b, j + n_ctx_tiles, 0, 0)),
                  pl.BlockSpec((1, 1, 6, d), lambda b, j: (b, 1, 0, 0)),
                  pl.BlockSpec(nw.shape, lambda b, j: (0, 0))],
        out_specs=pl.BlockSpec((1, TILE, d), lambda b, j: (b, j, 0)),
        compiler_params=pltpu.CompilerParams(dimension_semantics=("parallel", "parallel"),
                                             vmem_limit_bytes=VMEM_LIMIT),
        name="final_norm",
    )(xb, ys, posw, modx, nw)


def kernel(x, c, ctx, c_ctx, norm1_w, norm2_w, w_ada, b_ada, w_in, hgrn_lb_logits, hgrn_gnorm_w, w_pool,
           pool_scale, w_out, w_router, router_bias, w1_experts, w3_experts, w2_experts, w1_shared,
           w3_shared, w2_shared, final_norm_w):
    bsz, seq_len, d = x.shape
    ctx_len = ctx.shape[1]
    depth = w_ada.shape[0]
    assert ctx_len == TILE and seq_len % TILE == 0 and seq_len % GRID_W == 0
    ntok = ctx_len + seq_len
    ntile = bsz * ntok // TILE
    nb_max = -(-ntile * MAX_GRAN_PER_TILE // GRAN_PER_BLOCK) + N_EXPERTS

    lf, lbm = _level_matrices()
    lower, upper = _tri_matrices()
    pa, pinv = _pool_matrices()
    consts = (jnp.asarray(np.stack([lower, upper]), BF16), jnp.asarray(np.maximum(lf, lbm)),
              jnp.asarray(pa, BF16), jnp.asarray(pinv))

    p_lb = jax.nn.softmax(hgrn_lb_logits.astype(F32), axis=0)
    lb_all = jnp.cumsum(p_lb, axis=0) - p_lb[0:1]

    rows = 16
    cvec = jnp.concatenate([c, c_ctx[None, :], jnp.zeros((rows - bsz - 1, d), F32)], axis=0)
    mod = _ada_table(cvec, w_ada, b_ada)
    mod_lat = mod[:, :bsz].reshape(depth, bsz, 6, d)
    mod_ctx = jnp.broadcast_to(mod[:, bsz].reshape(depth, 1, 6, d), (depth, bsz, 6, d))
    modx = jnp.stack([mod_ctx, mod_lat], axis=2)

    xs = (jnp.concatenate([ctx, x], axis=1),)
    for l in range(depth):
        has_prev = l > 0
        outs = _mixer_in(has_prev, xs + ((modx[l - 1],) if has_prev else ()), modx[l], norm1_w[l][None, :],
                         w_in[l].astype(BF16), lb_all[l], consts, w_pool[l].astype(BF16),
                         pool_scale[l][None, :])
        oi, qe, ut, a, og, po = outs[:6]
        x_cur = outs[6] if has_prev else xs[0]
        sf, sb = _scan_state(ut, a)
        wr_t = w_router[l].T
        wr_hi = wr_t.astype(BF16)
        xb, rows_sorted, posw, cnt = _mixer_out(
            x_cur, oi, qe, sf, sb, og, po, modx[l], hgrn_gnorm_w[l][None, :], w_out[l].astype(BF16),
            norm2_w[l][None, :], jnp.stack([wr_hi, (wr_t - wr_hi.astype(F32)).astype(BF16)]),
            router_bias[l][:, None], w1_shared[l].astype(BF16),
            w3_shared[l].astype(BF16), w2_shared[l].astype(BF16), consts[0])
        plan = _moe_plan(cnt[:, :, :, 0].reshape(ntile, N_EXPERTS).astype(jnp.int32), nb_max)
        ys = _moe_ffn(rows_sorted, plan, w1_experts[l].astype(BF16), w3_experts[l].astype(BF16),
                      w2_experts[l].astype(BF16), nb_max)
        xs = (xb, ys, posw)
    return _final(xs[0], xs[1], xs[2], modx[depth - 1], final_norm_w[None, :], ctx_len // TILE)
```

```python
import functools

import numpy as np
import jax
import jax.numpy as jnp
from jax import lax
from jax.experimental import pallas as pl
from jax.experimental.pallas import tpu as pltpu

F32 = jnp.float32
BF16 = jnp.bfloat16

EPS = 1e-6
LB_FLOOR = 1e-30
MASK_SCORE = -1e4
GRID_W = 64
HEAD_DIM = 128
N_HEADS = 4
HGRN_W = N_HEADS * HEAD_DIM
POOL_WINDOWS = (2, 4, 8, 16)
POOL_GROUPS = len(POOL_WINDOWS)
POOL_GROUP_DIM = 128
POOL_W = POOL_GROUPS * POOL_GROUP_DIM
N_EXPERTS = 64
TOP_K = 6
N_EXPERT_GROUPS = 8
EXPERTS_PER_GROUP = N_EXPERTS // N_EXPERT_GROUPS
TOPK_GROUPS = 4
ROUTED_SCALE = 2.5

TILE = 256
N_LEVELS = 8
GRAN = 16
MAX_GRAN_PER_TILE = (TILE * TOP_K + N_EXPERTS * (GRAN - 1)) // GRAN
GRAN_PER_TILE = -(-MAX_GRAN_PER_TILE * GRAN // TILE) * TILE // GRAN
SORT_ROWS = GRAN_PER_TILE * GRAN
GRAN_PER_BLOCK = 32
VMEM_LIMIT = 56 * 1024 * 1024
NEG_INF = float("-inf")
LOG2E = 1.4426950408889634


def _sigmoid(z):
    return 1.0 / (1.0 + jnp.exp(-z))


def _silu(z):
    return z * _sigmoid(z)


def _dot(a, b):
    return jnp.dot(a, b, preferred_element_type=F32)


def _dot_nt(a, b):
    return lax.dot_general(a, b, (((1,), (1,)), ((), ())), preferred_element_type=F32)


def _dot_tn(a, b):
    return lax.dot_general(a, b, (((0,), (0,)), ((), ())), preferred_element_type=F32)


def _split3(g):
    hi = g.astype(BF16)
    r1 = g - hi.astype(F32)
    mid = r1.astype(BF16)
    lo = (r1 - mid.astype(F32)).astype(BF16)
    return hi, mid, lo


def _rmsnorm(x, w):
    return x * lax.rsqrt(jnp.mean(x * x, axis=-1, keepdims=True) + EPS) * w


def _level_matrices():
    t = np.arange(TILE)
    xor = t[:, None] ^ t[None, :]
    lvl = np.floor(np.log2(np.maximum(xor, 1))).astype(np.int32)
    lf = np.where(t[:, None] > t[None, :], lvl, -1).astype(np.int32)
    return lf, lf.T.copy()


def _tri_matrices():
    t = np.arange(TILE)
    lower = (t[:, None] >= t[None, :]).astype(np.float32)
    return lower, lower.T.copy()


def _pool_matrices():
    a = np.zeros((2, POOL_GROUPS, TILE, TILE), np.float32)
    inv = np.zeros((2, TILE, POOL_W), np.float32)
    t = np.arange(TILE)
    for kind, n in ((0, TILE), (1, GRID_W)):
        pos, base = t % n, (t // n) * n
        for gi, w in enumerate(POOL_WINDOWS):
            lo = np.clip(pos - w // 2, 0, n)
            hi = np.clip(pos + w - w // 2, 0, n)
            s = t[None, :]
            a[kind, gi] = ((s >= (base + lo)[:, None]) & (s < (base + hi)[:, None])).astype(np.float32)
            inv[kind, :, gi * POOL_GROUP_DIM:(gi + 1) * POOL_GROUP_DIM] = (1.0 / (hi - lo))[:, None]
    return a, inv


def _ada_kernel(c_ref, w_ref, b_ref, o_ref):
    o_ref[0] = jnp.dot(_silu(c_ref[...]), w_ref[0], preferred_element_type=F32,
                       precision=lax.Precision.HIGHEST) + b_ref[0]


def _ada_table(cvec, w_ada, b_ada):
    depth, d, n = w_ada.shape
    nb = 4
    rows = cvec.shape[0]
    return pl.pallas_call(
        _ada_kernel,
        out_shape=jax.ShapeDtypeStruct((depth, rows, n), F32),
        grid=(depth, nb),
        in_specs=[pl.BlockSpec((rows, d), lambda l, j: (0, 0)),
                  pl.BlockSpec((1, d, n // nb), lambda l, j: (l, 0, j)),
                  pl.BlockSpec((1, 1, n // nb), lambda l, j: (l, 0, j))],
        out_specs=pl.BlockSpec((1, rows, n // nb), lambda l, j: (l, 0, j)),
        compiler_params=pltpu.CompilerParams(dimension_semantics=("arbitrary", "arbitrary"),
                                             vmem_limit_bytes=VMEM_LIMIT),
        name="ada_table",
    )(cvec, w_ada, b_ada.reshape(depth, 1, n))


def _reference_rows(b, lvl, backward):
    w = 1 << lvl
    pm = w if backward else w - 1
    if w >= 4:
        n = TILE // (2 * w)
        row = b.reshape(n, 2 * w, HEAD_DIM)[:, pm:pm + 1, :]
        return jnp.broadcast_to(row, (n, 2 * w, HEAD_DIM)).reshape(TILE, HEAD_DIM)
    pos = lax.broadcasted_iota(jnp.int32, (TILE, HEAD_DIM), 0) & (2 * w - 1)
    out = b
    for p in range(2 * w):
        if p == pm:
            continue
        out = jnp.where(pos == p, pltpu.roll(b, (p - pm) % TILE, axis=0), out)
    return out


def _block_positions(posw, rb):
    rel = posw[0:8, :] - float(rb * TILE)
    rel = jnp.where((rel >= 0.0) & (rel < float(TILE)), rel, -1.0).astype(BF16)
    return [rel[k:k + 1, :] for k in range(TOP_K)]


def _combine(ys_ref, posw):
    row = lax.broadcasted_iota(jnp.int32, (TILE, TILE), 0).astype(F32).astype(BF16)
    wts = posw[8:16, :].astype(BF16)
    zero = jnp.zeros((TILE, TILE), BF16)
    acc = None
    for rb in range(SORT_ROWS // TILE):
        pw = zero
        for k, rel in enumerate(_block_positions(posw, rb)):
            pw = pw + jnp.where(row == rel, wts[k:k + 1, :], zero)
        part = _dot_tn(pw, ys_ref[rb * TILE:(rb + 1) * TILE, :])
        acc = part if acc is None else acc + part
    return acc


def _mixer_in_kernel(has_prev, *refs):
    if has_prev:
        xb_ref, ys_ref, posw_ref, modp_ref = refs[:4]
        refs = refs[4:]
    else:
        x_ref = refs[0]
        refs = refs[1:]
    (mod_ref, nw_ref, win_ref, lb_ref, tri_ref, lvl_ref, pa_ref, pinv_ref, wp_ref, ps_ref) = refs[:10]
    outs = refs[10:]
    if has_prev:
        oi_ref, qe_ref, ut_ref, a_ref, og_ref, po_ref, xo_ref = outs
        x = xb_ref[0] + modp_ref[0, 0][5:6] * _combine(ys_ref, posw_ref[0, 0])
        xo_ref[0] = x
    else:
        oi_ref, qe_ref, ut_ref, a_ref, og_ref, po_ref = outs
        x = x_ref[0]

    m = mod_ref[0, 0]
    hx = (_rmsnorm(x, nw_ref[...]) * (1.0 + m[1:2]) + m[0:1]).astype(BF16)

    def proj(col):
        return _dot(hx, win_ref[:, col * HGRN_W:(col + 1) * HGRN_W])

    v = proj(2).astype(BF16)
    q = _silu(proj(3)) * (HEAD_DIM ** -0.5)
    og_ref[0] = _silu(proj(4)).astype(BF16)

    p = proj(5)
    pb = p.astype(BF16)
    pinv = pinv_ref[0]
    for g in range(POOL_GROUPS):
        sl = slice(g * POOL_GROUP_DIM, (g + 1) * POOL_GROUP_DIM)
        pooled = _dot(pa_ref[0, g], pb[:, sl]) * pinv[:, sl] - p[:, sl]
        po_ref[0, :, sl] = (_dot(pooled.astype(BF16), wp_ref[g]) * ps_ref[:, sl]).astype(BF16)

    ks, bs = [], []
    for d in range(2):
        z = proj(d)
        lb = lb_ref[d:d + 1, :]
        sg = _sigmoid(z)
        kk = (1.0 - lb) * (1.0 - sg)
        g = jnp.log(jnp.maximum(lb, LB_FLOOR) + (1.0 - lb) * sg)
        hi, mid, lo = _split3(g)
        tri = tri_ref[d]
        b = _dot(tri, hi) + _dot(tri, mid) + _dot(tri, lo)
        b_tot = b[TILE - 1:TILE, :] if d == 0 else b[0:1, :]
        qe_ref[0, :, d * HGRN_W:(d + 1) * HGRN_W] = (q * jnp.exp(b)).astype(BF16)
        k_end = (kk * jnp.exp(b_tot - b)).astype(BF16)
        a_tot = jnp.exp(b_tot)
        for h in range(N_HEADS):
            sl = slice(h * HEAD_DIM, (h + 1) * HEAD_DIM)
            ut_ref[0, 0, d, h] = _dot_tn(v[:, sl], k_end[:, sl])
            a_ref[0, 0, d, h] = a_tot[:, sl]
        ks.append(kk)
        bs.append(b)

    rows = lax.broadcasted_iota(jnp.int32, (TILE, HEAD_DIM), 0)
    upper_f32 = [((rows >> lvl) & 1) == 1 for lvl in range(N_LEVELS)]
    upper_b16 = [jnp.where(u, 1.0, 0.0).astype(BF16) > 0 for u in upper_f32]
    eye = (lax.broadcasted_iota(jnp.int32, (TILE, TILE), 0)
           == lax.broadcasted_iota(jnp.int32, (TILE, TILE), 1))
    lmat = lvl_ref[...]
    zero = jnp.zeros((TILE, HEAD_DIM), BF16)
    for h in range(N_HEADS):
        sl = slice(h * HEAD_DIM, (h + 1) * HEAD_DIM)
        qh = q[:, sl]
        qb, kf, kb = qh.astype(BF16), ks[0][:, sl].astype(BF16), ks[1][:, sl].astype(BF16)
        bf, bb = bs[0][:, sl] * LOG2E, bs[1][:, sl] * LOG2E
        diag = jnp.sum(qh * (ks[0][:, sl] + ks[1][:, sl]), axis=-1, keepdims=True)
        scores = jnp.where(eye, diag, 0.0)
        for lvl in range(N_LEVELS):
            up, upb = upper_f32[lvl], upper_b16[lvl]
            ref_f = _reference_rows(bf, lvl, False)
            ref_b = _reference_rows(bb, lvl, True)
            arg_f = jnp.where(up, bf - ref_f, ref_f - bf)
            arg_b = jnp.where(up, ref_b - bb, bb - ref_b)
            xf = jnp.where(upb, qb, kf) * jnp.exp2(arg_f.astype(BF16))
            xb = jnp.where(upb, kb, qb) * jnp.exp2(arg_b.astype(BF16))
            lhs = jnp.concatenate([jnp.where(upb, xf, zero), jnp.where(upb, zero, xb)], axis=-1)
            rhs = jnp.concatenate([xf, xb], axis=-1)
            scores = jnp.where(lmat == lvl, _dot_nt(lhs, rhs), scores)
        oi_ref[0, :, sl] = _dot(scores.astype(BF16), v[:, sl])


def _mixer_in(has_prev, xs, modx, nw, win, lb, consts, wp, ps):
    tri, lvlm, pa, pinv = consts
    bsz, ntok, d = xs[0].shape
    nt = ntok // TILE
    tok = lambda w: pl.BlockSpec((1, TILE, w), lambda b, j: (b, j, 0))
    modspec = pl.BlockSpec((1, 1, 6, d), lambda b, j: (b, jnp.minimum(j, 1), 0, 0))
    full = lambda a: pl.BlockSpec(a.shape, lambda b, j: (0,) * a.ndim)
    sorted_rows = pl.BlockSpec((SORT_ROWS, d), lambda b, j: (b * nt + j, 0))
    picks = pl.BlockSpec((1, 1, 16, TILE), lambda b, j: (b, j, 0, 0))
    in_specs = ([tok(d), sorted_rows, picks, modspec] if has_prev else [tok(d)]) + [
        modspec, full(nw), full(win), full(lb), full(tri), full(lvlm),
        pl.BlockSpec((1,) + pa.shape[1:], lambda b, j: (jnp.minimum(j, 1), 0, 0, 0)),
        pl.BlockSpec((1,) + pinv.shape[1:], lambda b, j: (jnp.minimum(j, 1), 0, 0)),
        full(wp), full(ps)]
    out_shape = [jax.ShapeDtypeStruct((bsz, ntok, HGRN_W), F32),
                 jax.ShapeDtypeStruct((bsz, ntok, 2 * HGRN_W), BF16),
                 jax.ShapeDtypeStruct((bsz, nt, 2, N_HEADS, HEAD_DIM, HEAD_DIM), F32),
                 jax.ShapeDtypeStruct((bsz, nt, 2, N_HEADS, 1, HEAD_DIM), F32),
                 jax.ShapeDtypeStruct((bsz, ntok, HGRN_W), BF16),
                 jax.ShapeDtypeStruct((bsz, ntok, POOL_W), BF16)]
    out_specs = [tok(HGRN_W), tok(2 * HGRN_W),
                 pl.BlockSpec((1, 1, 2, N_HEADS, HEAD_DIM, HEAD_DIM), lambda b, j: (b, j, 0, 0, 0, 0)),
                 pl.BlockSpec((1, 1, 2, N_HEADS, 1, HEAD_DIM), lambda b, j: (b, j, 0, 0, 0, 0)),
                 tok(HGRN_W), tok(POOL_W)]
    if has_prev:
        out_shape.append(jax.ShapeDtypeStruct((bsz, ntok, d), F32))
        out_specs.append(tok(d))
    args = list(xs) + [modx, nw, win, lb, tri, lvlm, pa, pinv, wp, ps]
    return pl.pallas_call(
        functools.partial(_mixer_in_kernel, has_prev),
        out_shape=out_shape, grid=(bsz, nt), in_specs=in_specs, out_specs=out_specs,
        compiler_params=pltpu.CompilerParams(dimension_semantics=("parallel", "parallel"),
                                             vmem_limit_bytes=VMEM_LIMIT),
        name="mixer_in",
    )(*args)


def _scan_kernel(utf_ref, af_ref, utb_ref, ab_ref, sf_ref, sb_ref, accf, accb):
    @pl.when(pl.program_id(1) == 0)
    def _():
        accf[...] = jnp.zeros_like(accf)
        accb[...] = jnp.zeros_like(accb)

    sf_ref[0, 0] = accf[...]
    sb_ref[0, 0] = accb[...]
    accf[...] = af_ref[0, 0, 0] * accf[...] + utf_ref[0, 0, 0]
    accb[...] = ab_ref[0, 0, 0] * accb[...] + utb_ref[0, 0, 0]


def _scan_state(ut, a):
    bsz, nt = ut.shape[:2]
    bwd = lambda s: jnp.where(s == 0, 0, nt - s)
    st = (N_HEADS, HEAD_DIM, HEAD_DIM)
    ut_spec = lambda d, o: pl.BlockSpec((1, 1, 1) + st, lambda b, s: (b, o(s), d, 0, 0, 0))
    a_spec = lambda d, o: pl.BlockSpec((1, 1, 1, N_HEADS, 1, HEAD_DIM), lambda b, s: (b, o(s), d, 0, 0, 0))
    ident = lambda s: s
    return pl.pallas_call(
        _scan_kernel,
        out_shape=[jax.ShapeDtypeStruct((bsz, nt) + st, F32)] * 2,
        grid=(bsz, nt),
        in_specs=[ut_spec(0, ident), a_spec(0, ident), ut_spec(1, bwd), a_spec(1, bwd)],
        out_specs=[pl.BlockSpec((1, 1) + st, lambda b, s: (b, s, 0, 0, 0)),
                   pl.BlockSpec((1, 1) + st, lambda b, s: (b, bwd(s), 0, 0, 0))],
        scratch_shapes=[pltpu.VMEM(st, F32), pltpu.VMEM(st, F32)],
        compiler_params=pltpu.CompilerParams(dimension_semantics=("parallel", "arbitrary"),
                                             vmem_limit_bytes=VMEM_LIMIT),
        name="scan_state",
    )(ut, a, ut, a)


def _route(scores, bias):
    e, t = scores.shape
    ng, per = N_EXPERT_GROUPS, EXPERTS_PER_GROUP
    biased = scores + bias
    grouped = biased.reshape(ng, per, t)
    member = lax.broadcasted_iota(jnp.int32, (ng, per, t), 1).astype(F32)
    m1 = jnp.max(grouped, axis=1, keepdims=True)
    i1 = jnp.min(jnp.where(grouped == m1, member, float(per)), axis=1, keepdims=True)
    m2 = jnp.max(jnp.where(member == i1, NEG_INF, grouped), axis=1, keepdims=True)
    work = jnp.broadcast_to(m1 + m2, (ng, per, t)).reshape(e, t)
    eidx = lax.broadcasted_iota(jnp.int32, (e, t), 0)
    gidx = (eidx >> 3).astype(F32)
    eidx = eidx.astype(F32)

    def first_max(x, idx):
        mx = jnp.max(x, axis=0, keepdims=True)
        return jnp.min(jnp.where(x == mx, idx, float(4 * e)), axis=0, keepdims=True)

    keep = jnp.zeros_like(biased)
    for _ in range(TOPK_GROUPS):
        hit = gidx == first_max(work, gidx)
        keep = jnp.where(hit, 1.0, keep)
        work = jnp.where(hit, NEG_INF, work)
    work = jnp.where(keep > 0.0, biased, MASK_SCORE)
    sel = jnp.zeros_like(biased)
    hits = []
    for _ in range(TOP_K):
        hit = eidx == first_max(work, eidx)
        hits.append(hit)
        sel = jnp.where(hit, 1.0, sel)
        work = jnp.where(hit, NEG_INF, work)
    gate = jnp.where(sel > 0.0, scores, 0.0)
    return gate / jnp.sum(gate, axis=0, keepdims=True) * ROUTED_SCALE, sel, hits


def _sort_rows(gates, sel, hits, h2b, tri_upper, xs_ref, posw_ref, cnt_ref):
    e, t = sel.shape
    selb = sel.astype(BF16)
    incl = _dot(selb, tri_upper)
    cnt = _dot(selb, jnp.ones((t, 128), BF16))
    ngran = jnp.floor((cnt + float(GRAN - 1)) * (1.0 / GRAN))
    ei = lax.broadcasted_iota(jnp.int32, (e, e), 0)
    ej = lax.broadcasted_iota(jnp.int32, (e, e), 1)
    earlier = jnp.where(ej < ei, 1.0, 0.0).astype(BF16)
    run_start = _dot(earlier, ngran.astype(BF16)) * float(GRAN)
    slot = run_start[:, 0:1] + incl - sel
    cnt_ref[0, 0] = cnt

    ridx = lax.broadcasted_iota(jnp.int32, (16, t), 0)
    posw = jnp.where(ridx < 8, -1.0, 0.0)
    for k, hit in enumerate(hits):
        pos_k = jnp.sum(jnp.where(hit, slot, 0.0), axis=0, keepdims=True)
        w_k = jnp.sum(jnp.where(hit, gates, 0.0), axis=0, keepdims=True)
        posw = jnp.where(ridx == k, pos_k, jnp.where(ridx == 8 + k, w_k, posw))
    posw_ref[0, 0] = posw

    row = lax.broadcasted_iota(jnp.int32, (TILE, t), 0).astype(F32).astype(BF16)
    one, zero = jnp.ones((TILE, t), BF16), jnp.zeros((TILE, t), BF16)
    for rb in range(SORT_ROWS // TILE):
        onehot = zero
        for rel in _block_positions(posw, rb):
            onehot = onehot + jnp.where(row == rel, one, zero)
        xs_ref[rb * TILE:(rb + 1) * TILE, :] = _dot(onehot, h2b).astype(BF16)


def _mixer_out_kernel(x_ref, oi_ref, qe_ref, sf_ref, sb_ref, og_ref, po_ref, mod_ref, gw_ref, wo_ref,
                      nw2_ref, wr_ref, rb_ref, sw1_ref, sw3_ref, sw2_ref, tri_ref,
                      xb_ref, xs_ref, posw_ref, cnt_ref):
    m = mod_ref[0, 0]
    parts = []
    for h in range(N_HEADS):
        sl = slice(h * HEAD_DIM, (h + 1) * HEAD_DIM)
        o = (oi_ref[0, :, sl]
             + _dot_nt(qe_ref[0, :, sl], sf_ref[0, 0, h].astype(BF16))
             + _dot_nt(qe_ref[0, :, HGRN_W + h * HEAD_DIM:HGRN_W + (h + 1) * HEAD_DIM],
                       sb_ref[0, 0, h].astype(BF16)))
        y = _rmsnorm(o, gw_ref[...]) * og_ref[0, :, sl].astype(F32)
        parts.append(y.astype(BF16))
    mix = jnp.concatenate(parts + [po_ref[0]], axis=-1)
    x1 = x_ref[0] + m[2:3] * _dot(mix, wo_ref[...])
    h2 = _rmsnorm(x1, nw2_ref[...]) * (1.0 + m[4:5]) + m[3:4]
    h2b = h2.astype(BF16)
    shared = _dot((_silu(_dot(h2b, sw1_ref[...])) * _dot(h2b, sw3_ref[...])).astype(BF16), sw2_ref[...])
    xb_ref[0] = x1 + m[5:6] * shared
    h2lo = (h2 - h2b.astype(F32)).astype(BF16)
    logits = _dot_nt(wr_ref[0], h2b) + _dot_nt(wr_ref[0], h2lo) + _dot_nt(wr_ref[1], h2b)
    gates, sel, hits = _route(_sigmoid(logits), rb_ref[...])
    _sort_rows(gates, sel, hits, h2b, tri_ref[1], xs_ref, posw_ref, cnt_ref)


def _mixer_out(x, oi, qe, sf, sb, og, po, modx, gw, wo, nw2, wr, rb, sw1, sw3, sw2, tri):
    bsz, ntok, d = x.shape
    nt = ntok // TILE
    tok = lambda w: pl.BlockSpec((1, TILE, w), lambda b, j: (b, j, 0))
    st = pl.BlockSpec((1, 1, N_HEADS, HEAD_DIM, HEAD_DIM), lambda b, j: (b, j, 0, 0, 0))
    full = lambda a: pl.BlockSpec(a.shape, lambda b, j: (0,) * a.ndim)
    return pl.pallas_call(
        _mixer_out_kernel,
        out_shape=[jax.ShapeDtypeStruct((bsz, ntok, d), F32),
                   jax.ShapeDtypeStruct((bsz * nt * SORT_ROWS, d), BF16),
                   jax.ShapeDtypeStruct((bsz, nt, 16, TILE), F32),
                   jax.ShapeDtypeStruct((bsz, nt, N_EXPERTS, 128), F32)],
        grid=(bsz, nt),
        in_specs=[tok(d), tok(HGRN_W), tok(2 * HGRN_W), st, st, tok(HGRN_W), tok(POOL_W),
                  pl.BlockSpec((1, 1, 6, d), lambda b, j: (b, jnp.minimum(j, 1), 0, 0)),
                  full(gw), full(wo), full(nw2), full(wr), full(rb), full(sw1), full(sw3), full(sw2),
                  full(tri)],
        out_specs=[tok(d), pl.BlockSpec((SORT_ROWS, d), lambda b, j: (b * nt + j, 0)),
                   pl.BlockSpec((1, 1, 16, TILE), lambda b, j: (b, j, 0, 0)),
                   pl.BlockSpec((1, 1, N_EXPERTS, 128), lambda b, j: (b, j, 0, 0))],
        compiler_params=pltpu.CompilerParams(dimension_semantics=("parallel", "parallel"),
                                             vmem_limit_bytes=VMEM_LIMIT),
        name="mixer_out",
    )(x, oi, qe, sf, sb, og, po, modx, gw, wo, nw2, wr, rb, sw1, sw3, sw2, tri)


def _moe_plan(cnt, nb_max):
    ntile, ne = cnt.shape
    ngran = (cnt + GRAN - 1) // GRAN
    run_start = jnp.cumsum(ngran, axis=1) - ngran
    ngran_t = ngran.T
    per_e = ngran_t.sum(axis=1)
    nblk = (per_e + GRAN_PER_BLOCK - 1) // GRAN_PER_BLOCK
    blk_end = jnp.cumsum(nblk)
    blk_start = blk_end - nblk
    nb = blk_end[-1]
    bidx = jnp.arange(nb_max, dtype=jnp.int32)
    be = jnp.sum((blk_end[None, :] <= jnp.minimum(bidx, nb - 1)[:, None]).astype(jnp.int32), axis=1)
    q0 = (bidx - blk_start[be]) * GRAN_PER_BLOCK
    nvalid = jnp.where(bidx < nb, jnp.clip(per_e[be] - q0, 0, GRAN_PER_BLOCK), 0)
    run_hi = jnp.cumsum(ngran_t, axis=1)
    run_lo = run_hi - ngran_t
    base = jnp.arange(ntile, dtype=jnp.int32)[None, :] * GRAN_PER_TILE + run_start.T - run_lo
    q = (q0[:, None] + jnp.arange(GRAN_PER_BLOCK, dtype=jnp.int32)[None, :])[:, :, None]
    in_run = (run_lo[be][:, None, :] <= q) & (q < run_hi[be][:, None, :])
    src = jnp.sum(jnp.where(in_run, base[be][:, None, :] + q, 0), axis=-1)
    src = jnp.where(jnp.any(in_run, axis=-1), src, src[:, 0:1])
    return (nvalid.astype(jnp.int32), be, nb.reshape(1).astype(jnp.int32),
            src.reshape(-1).astype(jnp.int32))


def _moe_ffn_kernel(nv_ref, be_ref, nb_ref, src_ref, xs_hbm, w1_ref, w3_ref, w2_ref, ys_hbm,
                    xin, yout, sem_in, sem_out):
    b = pl.program_id(0)
    nb = nb_ref[0]
    slot = lax.rem(b, 2)
    block_rows = GRAN_PER_BLOCK * GRAN

    def start_gather(blk, sl):
        for i in range(GRAN_PER_BLOCK):
            row = pl.multiple_of(src_ref[blk * GRAN_PER_BLOCK + i] * GRAN, GRAN)
            pltpu.make_async_copy(xs_hbm.at[pl.ds(row, GRAN), :], xin.at[sl, pl.ds(i * GRAN, GRAN), :],
                                  sem_in.at[sl]).start()

    def wait_gather(sl):
        pltpu.make_async_copy(xs_hbm.at[pl.ds(0, block_rows), :], xin.at[sl], sem_in.at[sl]).wait()

    def granule_out(blk, sl, i):
        row = pl.multiple_of(src_ref[blk * GRAN_PER_BLOCK + i] * GRAN, GRAN)
        return pltpu.make_async_copy(yout.at[sl, pl.ds(i * GRAN, GRAN), :], ys_hbm.at[pl.ds(row, GRAN), :],
                                     sem_out.at[sl])

    def scatter(blk, sl, wait):
        nv = nv_ref[blk]

        @pl.when(nv == GRAN_PER_BLOCK)
        def _():
            if wait:
                pltpu.make_async_copy(yout.at[sl], ys_hbm.at[pl.ds(0, block_rows), :], sem_out.at[sl]).wait()
            else:
                for i in range(GRAN_PER_BLOCK):
                    granule_out(blk, sl, i).start()

        @pl.when(nv < GRAN_PER_BLOCK)
        def _():
            for i in range(GRAN_PER_BLOCK):
                @pl.when(i < nv)
                def _():
                    if wait:
                        granule_out(blk, sl, i).wait()
                    else:
                        granule_out(blk, sl, i).start()

    @pl.when(b < nb)
    def _():
        @pl.when(b == 0)
        def _():
            start_gather(b, slot)

        @pl.when(b + 1 < nb)
        def _():
            start_gather(b + 1, 1 - slot)

        wait_gather(slot)

        @pl.when(b >= 2)
        def _():
            scatter(b - 2, slot, True)

        x = xin[slot]
        hid = (_silu(_dot(x, w1_ref[0])) * _dot(x, w3_ref[0])).astype(BF16)
        yout[slot] = _dot(hid, w2_ref[0]).astype(BF16)
        scatter(b, slot, False)

        @pl.when(b == nb - 1)
        def _():
            scatter(b, slot, True)

            @pl.when(b >= 1)
            def _():
                scatter(b - 1, 1 - slot, True)


def _moe_ffn(xs, plan, w1, w3, w2, nb_max):
    nrows, d = xs.shape
    ne, _, f = w1.shape
    rows = GRAN_PER_BLOCK * GRAN
    wspec = lambda shape: pl.BlockSpec((1,) + shape, lambda b, nv, be, nb, src: (be[b], 0, 0))
    return pl.pallas_call(
        _moe_ffn_kernel,
        out_shape=jax.ShapeDtypeStruct((nrows, d), BF16),
        grid_spec=pltpu.PrefetchScalarGridSpec(
            num_scalar_prefetch=4, grid=(nb_max,),
            in_specs=[pl.BlockSpec(memory_space=pl.ANY), wspec((d, f)), wspec((d, f)), wspec((f, d))],
            out_specs=pl.BlockSpec(memory_space=pl.ANY),
            scratch_shapes=[pltpu.VMEM((2, rows, d), BF16), pltpu.VMEM((2, rows, d), BF16),
                            pltpu.SemaphoreType.DMA((2,)), pltpu.SemaphoreType.DMA((2,))]),
        input_output_aliases={4: 0},
        compiler_params=pltpu.CompilerParams(dimension_semantics=("arbitrary",),
                                             vmem_limit_bytes=VMEM_LIMIT),
        name="moe_ffn",
    )(*plan, xs, w1, w3, w2)


def _final_kernel(xb_ref, ys_ref, posw_ref, mod_ref, nw_ref, o_ref):
    x = xb_ref[0] + mod_ref[0, 0][5:6] * _combine(ys_ref, posw_ref[0, 0])
    o_ref[0] = _rmsnorm(x, nw_ref[...])


def _final(xb, ys, posw, modx, nw, n_ctx_tiles):
    bsz, ntok, d = xb.shape
    nt_all = ntok // TILE
    nt = nt_all - n_ctx_tiles
    lat = lambda w: pl.BlockSpec((1, TILE, w), lambda b, j: (b, j + n_ctx_tiles, 0))
    return pl.pallas_call(
        _final_kernel,
        out_shape=jax.ShapeDtypeStruct((bsz, nt * TILE, d), F32),
        grid=(bsz, nt),
        in_specs=[lat(d), pl.BlockSpec((SORT_ROWS, d), lambda b, j: (b * nt_all + j + n_ctx_tiles, 0)),
                  pl.BlockSpec((1, 1, 16, TILE), lambda b, j: (b, j + n_ctx_tiles, 0, 0)),
                  pl.BlockSpec((1, 1, 6, d), lambda b, j: (b, 1, 0, 0)),
                  pl.BlockSpec(nw.shape, lambda b, j: (0, 0))],
        out_specs=pl.BlockSpec((1, TILE, d), lambda b, j: (b, j, 0)),
        compiler_params=pltpu.CompilerParams(dimension_semantics=("parallel", "parallel"),
                                             vmem_limit_bytes=VMEM_LIMIT),
        name="final_norm",
    )(xb, ys, posw, modx, nw)


def kernel(x, c, ctx, c_ctx, norm1_w, norm2_w, w_ada, b_ada, w_in, hgrn_lb_logits, hgrn_gnorm_w, w_pool,
           pool_scale, w_out, w_router, router_bias, w1_experts, w3_experts, w2_experts, w1_shared,
           w3_shared, w2_shared, final_norm_w):
    bsz, seq_len, d = x.shape
    ctx_len = ctx.shape[1]
    depth = w_ada.shape[0]
    assert ctx_len == TILE and seq_len % TILE == 0 and seq_len % GRID_W == 0
    ntok = ctx_len + seq_len
    ntile = bsz * ntok // TILE
    nb_max = -(-ntile * MAX_GRAN_PER_TILE // GRAN_PER_BLOCK) + N_EXPERTS

    lf, lbm = _level_matrices()
    lower, upper = _tri_matrices()
    pa, pinv = _pool_matrices()
    consts = (jnp.asarray(np.stack([lower, upper]), BF16), jnp.asarray(np.maximum(lf, lbm)),
              jnp.asarray(pa, BF16), jnp.asarray(pinv))

    p_lb = jax.nn.softmax(hgrn_lb_logits.astype(F32), axis=0)
    lb_all = jnp.cumsum(p_lb, axis=0) - p_lb[0:1]

    rows = 16
    cvec = jnp.concatenate([c, c_ctx[None, :], jnp.zeros((rows - bsz - 1, d), F32)], axis=0)
    mod = _ada_table(cvec, w_ada, b_ada)
    mod_lat = mod[:, :bsz].reshape(depth, bsz, 6, d)
    mod_ctx = jnp.broadcast_to(mod[:, bsz].reshape(depth, 1, 6, d), (depth, bsz, 6, d))
    modx = jnp.stack([mod_ctx, mod_lat], axis=2)

    xs = (jnp.concatenate([ctx, x], axis=1),)
    for l in range(depth):
        has_prev = l > 0
        outs = _mixer_in(has_prev, xs + ((modx[l - 1],) if has_prev else ()), modx[l], norm1_w[l][None, :],
                         w_in[l].astype(BF16), lb_all[l], consts, w_pool[l].astype(BF16),
                         pool_scale[l][None, :])
        oi, qe, ut, a, og, po = outs[:6]
        x_cur = outs[6] if has_prev else xs[0]
        sf, sb = _scan_state(ut, a)
        wr_t = w_router[l].T
        wr_hi = wr_t.astype(BF16)
        xb, rows_sorted, posw, cnt = _mixer_out(
            x_cur, oi, qe, sf, sb, og, po, modx[l], hgrn_gnorm_w[l][None, :], w_out[l].astype(BF16),
            norm2_w[l][None, :], jnp.stack([wr_hi, (wr_t - wr_hi.astype(F32)).astype(BF16)]),
            router_bias[l][:, None], w1_shared[l].astype(BF16),
            w3_shared[l].astype(BF16), w2_shared[l].astype(BF16), consts[0])
        plan = _moe_plan(cnt[:, :, :, 0].reshape(ntile, N_EXPERTS).astype(jnp.int32), nb_max)
        ys = _moe_ffn(rows_sorted, plan, w1_experts[l].astype(BF16), w3_experts[l].astype(BF16),
                      w2_experts[l].astype(BF16), nb_max)
        xs = (xb, ys, posw)
    return _final(xs[0], xs[1], xs[2], modx[depth - 1], final_norm_w[None, :], ctx_len // TILE)
```

```python
import functools

import numpy as np
import jax
import jax.numpy as jnp
from jax import lax
from jax.experimental import pallas as pl
from jax.experimental.pallas import tpu as pltpu

F32 = jnp.float32
BF16 = jnp.bfloat16

EPS = 1e-6
LB_FLOOR = 1e-30
MASK_SCORE = -1e4
GRID_W = 64
HEAD_DIM = 128
N_HEADS = 4
HGRN_W = N_HEADS * HEAD_DIM
POOL_WINDOWS = (2, 4, 8, 16)
POOL_GROUPS = len(POOL_WINDOWS)
POOL_GROUP_DIM = 128
POOL_W = POOL_GROUPS * POOL_GROUP_DIM
N_EXPERTS = 64
TOP_K = 6
N_EXPERT_GROUPS = 8
EXPERTS_PER_GROUP = N_EXPERTS // N_EXPERT_GROUPS
TOPK_GROUPS = 4
ROUTED_SCALE = 2.5

TILE = 256
N_LEVELS = 8
GRAN = 16
MAX_GRAN_PER_TILE = (TILE * TOP_K + N_EXPERTS * (GRAN - 1)) // GRAN
GRAN_PER_TILE = -(-MAX_GRAN_PER_TILE * GRAN // TILE) * TILE // GRAN
SORT_ROWS = GRAN_PER_TILE * GRAN
GRAN_PER_BLOCK = 32
VMEM_LIMIT = 56 * 1024 * 1024
NEG_INF = float("-inf")
LOG2E = 1.4426950408889634


def _sigmoid(z):
    return 1.0 / (1.0 + jnp.exp(-z))


def _silu(z):
    return z * _sigmoid(z)


def _dot(a, b):
    return jnp.dot(a, b, preferred_element_type=F32)


def _dot_nt(a, b):
    return lax.dot_general(a, b, (((1,), (1,)), ((), ())), preferred_element_type=F32)


def _dot_tn(a, b):
    return lax.dot_general(a, b, (((0,), (0,)), ((), ())), preferred_element_type=F32)


def _split3(g):
    hi = g.astype(BF16)
    r1 = g - hi.astype(F32)
    mid = r1.astype(BF16)
    lo = (r1 - mid.astype(F32)).astype(BF16)
    return hi, mid, lo


def _rmsnorm(x, w):
    return x * lax.rsqrt(jnp.mean(x * x, axis=-1, keepdims=True) + EPS) * w


def _level_matrices():
    t = np.arange(TILE)
    xor = t[:, None] ^ t[None, :]
    lvl = np.floor(np.log2(np.maximum(xor, 1))).astype(np.int32)
    lf = np.where(t[:, None] > t[None, :], lvl, -1).astype(np.int32)
    return lf, lf.T.copy()


def _tri_matrices():
    t = np.arange(TILE)
    lower = (t[:, None] >= t[None, :]).astype(np.float32)
    return lower, lower.T.copy()


def _pool_matrices():
    a = np.zeros((2, POOL_GROUPS, TILE, TILE), np.float32)
    inv = np.zeros((2, TILE, POOL_W), np.float32)
    t = np.arange(TILE)
    for kind, n in ((0, TILE), (1, GRID_W)):
        pos, base = t % n, (t // n) * n
        for gi, w in enumerate(POOL_WINDOWS):
            lo = np.clip(pos - w // 2, 0, n)
            hi = np.clip(pos + w - w // 2, 0, n)
            s = t[None, :]
            a[kind, gi] = ((s >= (base + lo)[:, None]) & (s < (base + hi)[:, None])).astype(np.float32)
            inv[kind, :, gi * POOL_GROUP_DIM:(gi + 1) * POOL_GROUP_DIM] = (1.0 / (hi - lo))[:, None]
    return a, inv


def _ada_kernel(c_ref, w_ref, b_ref, o_ref):
    o_ref[0] = jnp.dot(_silu(c_ref[...]), w_ref[0], preferred_element_type=F32,
                       precision=lax.Precision.HIGHEST) + b_ref[0]


def _ada_table(cvec, w_ada, b_ada):
    depth, d, n = w_ada.shape
    nb = 4
    rows = cvec.shape[0]
    return pl.pallas_call(
        _ada_kernel,
        out_shape=jax.ShapeDtypeStruct((depth, rows, n), F32),
        grid=(depth, nb),
        in_specs=[pl.BlockSpec((rows, d), lambda l, j: (0, 0)),
                  pl.BlockSpec((1, d, n // nb), lambda l, j: (l, 0, j)),
                  pl.BlockSpec((1, 1, n // nb), lambda l, j: (l, 0, j))],
        out_specs=pl.BlockSpec((1, rows, n // nb), lambda l, j: (l, 0, j)),
        compiler_params=pltpu.CompilerParams(dimension_semantics=("arbitrary", "arbitrary"),
                                             vmem_limit_bytes=VMEM_LIMIT),
        name="ada_table",
    )(cvec, w_ada, b_ada.reshape(depth, 1, n))


def _reference_rows(b, lvl, backward):
    w = 1 << lvl
    pm = w if backward else w - 1
    if w >= 4:
        n = TILE // (2 * w)
        row = b.reshape(n, 2 * w, HEAD_DIM)[:, pm:pm + 1, :]
        return jnp.broadcast_to(row, (n, 2 * w, HEAD_DIM)).reshape(TILE, HEAD_DIM)
    pos = lax.broadcasted_iota(jnp.int32, (TILE, HEAD_DIM), 0) & (2 * w - 1)
    out = b
    for p in range(2 * w):
        if p == pm:
            continue
        out = jnp.where(pos == p, pltpu.roll(b, (p - pm) % TILE, axis=0), out)
    return out


def _block_positions(posw, rb):
    rel = posw[0:8, :] - float(rb * TILE)
    rel = jnp.where((rel >= 0.0) & (rel < float(TILE)), rel, -1.0).astype(BF16)
    return [rel[k:k + 1, :] for k in range(TOP_K)]


def _combine(ys_ref, posw):
    row = lax.broadcasted_iota(jnp.int32, (TILE, TILE), 0).astype(F32).astype(BF16)
    wts = posw[8:16, :].astype(BF16)
    zero = jnp.zeros((TILE, TILE), BF16)
    acc = None
    for rb in range(SORT_ROWS // TILE):
        pw = zero
        for k, rel in enumerate(_block_positions(posw, rb)):
            pw = pw + jnp.where(row == rel, wts[k:k + 1, :], zero)
        part = _dot_tn(pw, ys_ref[rb * TILE:(rb + 1) * TILE, :])
        acc = part if acc is None else acc + part
    return acc


def _mixer_in_kernel(has_prev, *refs):
    if has_prev:
        xb_ref, ys_ref, posw_ref, modp_ref = refs[:4]
        refs = refs[4:]
    else:
        ctx_ref, x_ref = refs[:2]
        refs = refs[2:]
    (mod_ref, nw_ref, win_ref, lb_ref, tri_ref, lvl_ref, pa_ref, pinv_ref, wp_ref, ps_ref) = refs[:10]
    outs = refs[10:]
    if has_prev:
        oi_ref, qe_ref, ut_ref, a_ref, og_ref, po_ref, xo_ref = outs
        x = xb_ref[0] + modp_ref[0, 0][5:6] * _combine(ys_ref, posw_ref[0, 0])
        xo_ref[0] = x
    else:
        oi_ref, qe_ref, ut_ref, a_ref, og_ref, po_ref = outs
        x = _tile_of(ctx_ref, x_ref)

    m = mod_ref[0, 0]
    hx = (_rmsnorm(x, nw_ref[...]) * (1.0 + m[1:2]) + m[0:1]).astype(BF16)

    def proj(col):
        return _dot(hx, win_ref[:, col * HGRN_W:(col + 1) * HGRN_W])

    v = proj(2).astype(BF16)
    q = _silu(proj(3)) * (HEAD_DIM ** -0.5)
    og_ref[0] = _silu(proj(4)).astype(BF16)

    p = proj(5)
    pb = p.astype(BF16)
    pinv = pinv_ref[0]
    for g in range(POOL_GROUPS):
        sl = slice(g * POOL_GROUP_DIM, (g + 1) * POOL_GROUP_DIM)
        pooled = _dot(pa_ref[0, g], pb[:, sl]) * pinv[:, sl] - p[:, sl]
        po_ref[0, :, sl] = (_dot(pooled.astype(BF16), wp_ref[g]) * ps_ref[:, sl]).astype(BF16)

    ks, bs = [], []
    for d in range(2):
        z = proj(d)
        lb = lb_ref[d:d + 1, :]
        sg = _sigmoid(z)
        kk = (1.0 - lb) * (1.0 - sg)
        g = jnp.log(jnp.maximum(lb, LB_FLOOR) + (1.0 - lb) * sg)
        hi, mid, lo = _split3(g)
        tri = tri_ref[d]
        b = _dot(tri, hi) + _dot(tri, mid) + _dot(tri, lo)
        b_tot = b[TILE - 1:TILE, :] if d == 0 else b[0:1, :]
        qe_ref[0, :, d * HGRN_W:(d + 1) * HGRN_W] = (q * jnp.exp(b)).astype(BF16)
        k_end = (kk * jnp.exp(b_tot - b)).astype(BF16)
        a_tot = jnp.exp(b_tot)
        for h in range(N_HEADS):
            sl = slice(h * HEAD_DIM, (h + 1) * HEAD_DIM)
            ut_ref[0, 0, d, h] = _dot_tn(v[:, sl], k_end[:, sl])
            a_ref[0, 0, d, h] = a_tot[:, sl]
        ks.append(kk)
        bs.append(b)

    rows = lax.broadcasted_iota(jnp.int32, (TILE, HEAD_DIM), 0)
    upper_f32 = [((rows >> lvl) & 1) == 1 for lvl in range(N_LEVELS)]
    upper_b16 = [jnp.where(u, 1.0, 0.0).astype(BF16) > 0 for u in upper_f32]
    eye = (lax.broadcasted_iota(jnp.int32, (TILE, TILE), 0)
           == lax.broadcasted_iota(jnp.int32, (TILE, TILE), 1))
    lmat = lvl_ref[...]
    zero = jnp.zeros((TILE, HEAD_DIM), BF16)
    for h in range(N_HEADS):
        sl = slice(h * HEAD_DIM, (h + 1) * HEAD_DIM)
        qh = q[:, sl]
        qb, kf, kb = qh.astype(BF16), ks[0][:, sl].astype(BF16), ks[1][:, sl].astype(BF16)
        bf, bb = bs[0][:, sl] * LOG2E, bs[1][:, sl] * LOG2E
        diag = jnp.sum(qh * (ks[0][:, sl] + ks[1][:, sl]), axis=-1, keepdims=True)
        scores = jnp.where(eye, diag, 0.0)
        for lvl in range(N_LEVELS):
            up, upb = upper_f32[lvl], upper_b16[lvl]
            ref_f = _reference_rows(bf, lvl, False)
            ref_b = _reference_rows(bb, lvl, True)
            arg_f = jnp.where(up, bf - ref_f, ref_f - bf)
            arg_b = jnp.where(up, ref_b - bb, bb - ref_b)
            xf = jnp.where(upb, qb, kf) * jnp.exp2(arg_f.astype(BF16))
            xb = jnp.where(upb, kb, qb) * jnp.exp2(arg_b.astype(BF16))
            lhs = jnp.concatenate([jnp.where(upb, xf, zero), jnp.where(upb, zero, xb)], axis=-1)
            rhs = jnp.concatenate([xf, xb], axis=-1)
            scores = jnp.where(lmat == lvl, _dot_nt(lhs, rhs), scores)
        oi_ref[0, :, sl] = _dot(scores.astype(BF16), v[:, sl])


def _tile_of(ctx_ref, x_ref):
    return jnp.where(pl.program_id(1) == 0, ctx_ref[0], x_ref[0])


def _split_specs(d):
    return [pl.BlockSpec((1, TILE, d), lambda b, j: (b, 0, 0)),
            pl.BlockSpec((1, TILE, d), lambda b, j: (b, jnp.maximum(j - 1, 0), 0))]


def _mixer_in(has_prev, xs, modx, nw, win, lb, consts, wp, ps):
    tri, lvlm, pa, pinv = consts
    bsz, _, d = xs[0].shape
    ntok = xs[0].shape[1] if has_prev else xs[0].shape[1] + xs[1].shape[1]
    nt = ntok // TILE
    tok = lambda w: pl.BlockSpec((1, TILE, w), lambda b, j: (b, j, 0))
    modspec = pl.BlockSpec((1, 1, 6, d), lambda b, j: (b, jnp.minimum(j, 1), 0, 0))
    full = lambda a: pl.BlockSpec(a.shape, lambda b, j: (0,) * a.ndim)
    sorted_rows = pl.BlockSpec((SORT_ROWS, d), lambda b, j: (b * nt + j, 0))
    picks = pl.BlockSpec((1, 1, 16, TILE), lambda b, j: (b, j, 0, 0))
    in_specs = ([tok(d), sorted_rows, picks, modspec] if has_prev else _split_specs(d)) + [
        modspec, full(nw), full(win), full(lb), full(tri), full(lvlm),
        pl.BlockSpec((1,) + pa.shape[1:], lambda b, j: (jnp.minimum(j, 1), 0, 0, 0)),
        pl.BlockSpec((1,) + pinv.shape[1:], lambda b, j: (jnp.minimum(j, 1), 0, 0)),
        full(wp), full(ps)]
    out_shape = [jax.ShapeDtypeStruct((bsz, ntok, HGRN_W), F32),
                 jax.ShapeDtypeStruct((bsz, ntok, 2 * HGRN_W), BF16),
                 jax.ShapeDtypeStruct((bsz, nt, 2, N_HEADS, HEAD_DIM, HEAD_DIM), F32),
                 jax.ShapeDtypeStruct((bsz, nt, 2, N_HEADS, 1, HEAD_DIM), F32),
                 jax.ShapeDtypeStruct((bsz, ntok, HGRN_W), BF16),
                 jax.ShapeDtypeStruct((bsz, ntok, POOL_W), BF16)]
    out_specs = [tok(HGRN_W), tok(2 * HGRN_W),
                 pl.BlockSpec((1, 1, 2, N_HEADS, HEAD_DIM, HEAD_DIM), lambda b, j: (b, j, 0, 0, 0, 0)),
                 pl.BlockSpec((1, 1, 2, N_HEADS, 1, HEAD_DIM), lambda b, j: (b, j, 0, 0, 0, 0)),
                 tok(HGRN_W), tok(POOL_W)]
    if has_prev:
        out_shape.append(jax.ShapeDtypeStruct((bsz, ntok, d), F32))
        out_specs.append(tok(d))
    args = list(xs) + [modx, nw, win, lb, tri, lvlm, pa, pinv, wp, ps]
    return pl.pallas_call(
        functools.partial(_mixer_in_kernel, has_prev),
        out_shape=out_shape, grid=(bsz, nt), in_specs=in_specs, out_specs=out_specs,
        compiler_params=pltpu.CompilerParams(dimension_semantics=("parallel", "parallel"),
                                             vmem_limit_bytes=VMEM_LIMIT),
        name="mixer_in",
    )(*args)


def _scan_kernel(utf_ref, af_ref, utb_ref, ab_ref, sf_ref, sb_ref, accf, accb):
    @pl.when(pl.program_id(1) == 0)
    def _():
        accf[...] = jnp.zeros_like(accf)
        accb[...] = jnp.zeros_like(accb)

    sf_ref[0, 0] = accf[...]
    sb_ref[0, 0] = accb[...]
    accf[...] = af_ref[0, 0, 0] * accf[...] + utf_ref[0, 0, 0]
    accb[...] = ab_ref[0, 0, 0] * accb[...] + utb_ref[0, 0, 0]


def _scan_state(ut, a):
    bsz, nt = ut.shape[:2]
    bwd = lambda s: jnp.where(s == 0, 0, nt - s)
    st = (N_HEADS, HEAD_DIM, HEAD_DIM)
    ut_spec = lambda d, o: pl.BlockSpec((1, 1, 1) + st, lambda b, s: (b, o(s), d, 0, 0, 0))
    a_spec = lambda d, o: pl.BlockSpec((1, 1, 1, N_HEADS, 1, HEAD_DIM), lambda b, s: (b, o(s), d, 0, 0, 0))
    ident = lambda s: s
    return pl.pallas_call(
        _scan_kernel,
        out_shape=[jax.ShapeDtypeStruct((bsz, nt) + st, F32)] * 2,
        grid=(bsz, nt),
        in_specs=[ut_spec(0, ident), a_spec(0, ident), ut_spec(1, bwd), a_spec(1, bwd)],
        out_specs=[pl.BlockSpec((1, 1) + st, lambda b, s: (b, s, 0, 0, 0)),
                   pl.BlockSpec((1, 1) + st, lambda b, s: (b, bwd(s), 0, 0, 0))],
        scratch_shapes=[pltpu.VMEM(st, F32), pltpu.VMEM(st, F32)],
        compiler_params=pltpu.CompilerParams(dimension_semantics=("parallel", "arbitrary"),
                                             vmem_limit_bytes=VMEM_LIMIT),
        name="scan_state",
    )(ut, a, ut, a)


def _route(scores, bias):
    e, t = scores.shape
    ng, per = N_EXPERT_GROUPS, EXPERTS_PER_GROUP
    biased = scores + bias
    grouped = biased.reshape(ng, per, t)
    member = lax.broadcasted_iota(jnp.int32, (ng, per, t), 1).astype(F32)
    m1 = jnp.max(grouped, axis=1, keepdims=True)
    i1 = jnp.min(jnp.where(grouped == m1, member, float(per)), axis=1, keepdims=True)
    m2 = jnp.max(jnp.where(member == i1, NEG_INF, grouped), axis=1, keepdims=True)
    work = jnp.broadcast_to(m1 + m2, (ng, per, t)).reshape(e, t)
    eidx = lax.broadcasted_iota(jnp.int32, (e, t), 0)
    gidx = (eidx >> 3).astype(F32)
    eidx = eidx.astype(F32)

    def first_max(x, idx):
        mx = jnp.max(x, axis=0, keepdims=True)
        return jnp.min(jnp.where(x == mx, idx, float(4 * e)), axis=0, keepdims=True)

    keep = jnp.zeros_like(biased)
    for _ in range(TOPK_GROUPS):
        hit = gidx == first_max(work, gidx)
        keep = jnp.where(hit, 1.0, keep)
        work = jnp.where(hit, NEG_INF, work)
    work = jnp.where(keep > 0.0, biased, MASK_SCORE)
    sel = jnp.zeros_like(biased)
    hits = []
    for _ in range(TOP_K):
        hit = eidx == first_max(work, eidx)
        hits.append(hit)
        sel = jnp.where(hit, 1.0, sel)
        work = jnp.where(hit, NEG_INF, work)
    gate = jnp.where(sel > 0.0, scores, 0.0)
    return gate / jnp.sum(gate, axis=0, keepdims=True) * ROUTED_SCALE, sel, hits


def _sort_rows(gates, sel, hits, h2b, tri_upper, xs_ref, posw_ref, cnt_ref):
    e, t = sel.shape
    selb = sel.astype(BF16)
    incl = _dot(selb, tri_upper)
    cnt = _dot(selb, jnp.ones((t, 128), BF16))
    ngran = jnp.floor((cnt + float(GRAN - 1)) * (1.0 / GRAN))
    ei = lax.broadcasted_iota(jnp.int32, (e, e), 0)
    ej = lax.broadcasted_iota(jnp.int32, (e, e), 1)
    earlier = jnp.where(ej < ei, 1.0, 0.0).astype(BF16)
    run_start = _dot(earlier, ngran.astype(BF16)) * float(GRAN)
    slot = run_start[:, 0:1] + incl - sel
    cnt_ref[0, 0] = cnt

    ridx = lax.broadcasted_iota(jnp.int32, (16, t), 0)
    posw = jnp.where(ridx < 8, -1.0, 0.0)
    for k, hit in enumerate(hits):
        pos_k = jnp.sum(jnp.where(hit, slot, 0.0), axis=0, keepdims=True)
        w_k = jnp.sum(jnp.where(hit, gates, 0.0), axis=0, keepdims=True)
        posw = jnp.where(ridx == k, pos_k, jnp.where(ridx == 8 + k, w_k, posw))
    posw_ref[0, 0] = posw

    row = lax.broadcasted_iota(jnp.int32, (TILE, t), 0).astype(F32).astype(BF16)
    one, zero = jnp.ones((TILE, t), BF16), jnp.zeros((TILE, t), BF16)
    for rb in range(SORT_ROWS // TILE):
        onehot = zero
        for rel in _block_positions(posw, rb):
            onehot = onehot + jnp.where(row == rel, one, zero)
        xs_ref[rb * TILE:(rb + 1) * TILE, :] = _dot(onehot, h2b).astype(BF16)


def _mixer_out_kernel(split, *refs):
    x_in = _tile_of(refs[0], refs[1]) if split else refs[0][0]
    (oi_ref, qe_ref, sf_ref, sb_ref, og_ref, po_ref, mod_ref, gw_ref, wo_ref, nw2_ref, wr_ref, rb_ref,
     sw1_ref, sw3_ref, sw2_ref, tri_ref, xb_ref, xs_ref, posw_ref, cnt_ref) = refs[2 if split else 1:]
    m = mod_ref[0, 0]
    parts = []
    for h in range(N_HEADS):
        sl = slice(h * HEAD_DIM, (h + 1) * HEAD_DIM)
        o = (oi_ref[0, :, sl]
             + _dot_nt(qe_ref[0, :, sl], sf_ref[0, 0, h].astype(BF16))
             + _dot_nt(qe_ref[0, :, HGRN_W + h * HEAD_DIM:HGRN_W + (h + 1) * HEAD_DIM],
                       sb_ref[0, 0, h].astype(BF16)))
        y = _rmsnorm(o, gw_ref[...]) * og_ref[0, :, sl].astype(F32)
        parts.append(y.astype(BF16))
    mix = jnp.concatenate(parts + [po_ref[0]], axis=-1)
    x1 = x_in + m[2:3] * _dot(mix, wo_ref[...])
    h2 = _rmsnorm(x1, nw2_ref[...]) * (1.0 + m[4:5]) + m[3:4]
    h2b = h2.astype(BF16)
    shared = _dot((_silu(_dot(h2b, sw1_ref[...])) * _dot(h2b, sw3_ref[...])).astype(BF16), sw2_ref[...])
    xb_ref[0] = x1 + m[5:6] * shared
    h2lo = (h2 - h2b.astype(F32)).astype(BF16)
    logits = _dot_nt(wr_ref[0], h2b) + _dot_nt(wr_ref[0], h2lo) + _dot_nt(wr_ref[1], h2b)
    gates, sel, hits = _route(_sigmoid(logits), rb_ref[...])
    _sort_rows(gates, sel, hits, h2b, tri_ref[1], xs_ref, posw_ref, cnt_ref)


def _mixer_out(x_parts, oi, qe, sf, sb, og, po, modx, gw, wo, nw2, wr, rb, sw1, sw3, sw2, tri):
    split = len(x_parts) == 2
    bsz, ntok, _ = oi.shape
    d = x_parts[0].shape[-1]
    nt = ntok // TILE
    tok = lambda w: pl.BlockSpec((1, TILE, w), lambda b, j: (b, j, 0))
    st = pl.BlockSpec((1, 1, N_HEADS, HEAD_DIM, HEAD_DIM), lambda b, j: (b, j, 0, 0, 0))
    full = lambda a: pl.BlockSpec(a.shape, lambda b, j: (0,) * a.ndim)
    return pl.pallas_call(
        functools.partial(_mixer_out_kernel, split),
        out_shape=[jax.ShapeDtypeStruct((bsz, ntok, d), F32),
                   jax.ShapeDtypeStruct((bsz * nt * SORT_ROWS, d), BF16),
                   jax.ShapeDtypeStruct((bsz, nt, 16, TILE), F32),
                   jax.ShapeDtypeStruct((bsz, nt, N_EXPERTS, 128), F32)],
        grid=(bsz, nt),
        in_specs=(_split_specs(d) if split else [tok(d)]) + [
                  tok(HGRN_W), tok(2 * HGRN_W), st, st, tok(HGRN_W), tok(POOL_W),
                  pl.BlockSpec((1, 1, 6, d), lambda b, j: (b, jnp.minimum(j, 1), 0, 0)),
                  full(gw), full(wo), full(nw2), full(wr), full(rb), full(sw1), full(sw3), full(sw2),
                  full(tri)],
        out_specs=[tok(d), pl.BlockSpec((SORT_ROWS, d), lambda b, j: (b * nt + j, 0)),
                   pl.BlockSpec((1, 1, 16, TILE), lambda b, j: (b, j, 0, 0)),
                   pl.BlockSpec((1, 1, N_EXPERTS, 128), lambda b, j: (b, j, 0, 0))],
        compiler_params=pltpu.CompilerParams(dimension_semantics=("parallel", "parallel"),
                                             vmem_limit_bytes=VMEM_LIMIT),
        name="mixer_out",
    )(*x_parts, oi, qe, sf, sb, og, po, modx, gw, wo, nw2, wr, rb, sw1, sw3, sw2, tri)


def _moe_plan(cnt, nb_max):
    ntile, ne = cnt.shape
    ngran = (cnt + GRAN - 1) // GRAN
    run_start = jnp.cumsum(ngran, axis=1) - ngran
    ngran_t = ngran.T
    per_e = ngran_t.sum(axis=1)
    nblk = (per_e + GRAN_PER_BLOCK - 1) // GRAN_PER_BLOCK
    blk_end = jnp.cumsum(nblk)
    blk_start = blk_end - nblk
    nb = blk_end[-1]
    bidx = jnp.arange(nb_max, dtype=jnp.int32)
    be = jnp.sum((blk_end[None, :] <= jnp.minimum(bidx, nb - 1)[:, None]).astype(jnp.int32), axis=1)
    q0 = (bidx - blk_start[be]) * GRAN_PER_BLOCK
    nvalid = jnp.where(bidx < nb, jnp.clip(per_e[be] - q0, 0, GRAN_PER_BLOCK), 0)
    run_hi = jnp.cumsum(ngran_t, axis=1)
    run_lo = run_hi - ngran_t
    base = jnp.arange(ntile, dtype=jnp.int32)[None, :] * GRAN_PER_TILE + run_start.T - run_lo
    q = (q0[:, None] + jnp.arange(GRAN_PER_BLOCK, dtype=jnp.int32)[None, :])[:, :, None]
    in_run = (run_lo[be][:, None, :] <= q) & (q < run_hi[be][:, None, :])
    src = jnp.sum(jnp.where(in_run, base[be][:, None, :] + q, 0), axis=-1)
    src = jnp.where(jnp.any(in_run, axis=-1), src, src[:, 0:1])
    return (nvalid.astype(jnp.int32), be, nb.reshape(1).astype(jnp.int32),
            src.reshape(-1).astype(jnp.int32))


def _moe_ffn_kernel(nv_ref, be_ref, nb_ref, src_ref, xs_hbm, w1_ref, w3_ref, w2_ref, ys_hbm,
                    xin0, xin1, yout0, yout1, w1b, w3b, w2b, sem_in, sem_out):
    b = pl.program_id(0)
    nb = nb_ref[0]
    xin, yout = (xin0, xin1), (yout0, yout1)
    block_rows = GRAN_PER_BLOCK * GRAN
    full = lambda blk: nv_ref[blk] == GRAN_PER_BLOCK

    @pl.when((b == 0) | (be_ref[b] != be_ref[jnp.maximum(b - 1, 0)]))
    def _():
        w1b[...] = w1_ref[0, 0].astype(BF16)
        w3b[...] = w3_ref[0, 0].astype(BF16)
        w2b[...] = w2_ref[0, 0].astype(BF16)

    def start_gather(blk, sl):
        for i in range(GRAN_PER_BLOCK):
            pltpu.make_async_copy(xs_hbm.at[src_ref[blk * GRAN_PER_BLOCK + i]],
                                  xin[sl].at[pl.ds(i * GRAN, GRAN), :], sem_in.at[sl]).start()

    def wait_gather(sl):
        for i in range(GRAN_PER_BLOCK):
            pltpu.make_async_copy(xs_hbm.at[0], xin[sl].at[pl.ds(i * GRAN, GRAN), :], sem_in.at[sl]).wait()

    def granule_out(blk, sl, i):
        return pltpu.make_async_copy(yout[sl].at[pl.ds(i * GRAN, GRAN), :],
                                     ys_hbm.at[src_ref[blk * GRAN_PER_BLOCK + i]], sem_out.at[sl])

    def ffn(sl, lo, hi):
        x = xin[sl][lo:hi, :]
        hid = (_silu(_dot(x, w1b[...])) * _dot(x, w3b[...])).astype(BF16)
        yout[sl][lo:hi, :] = _dot(hid, w2b[...]).astype(BF16)

    def scatter(blk, sl, wait):
        nv = nv_ref[blk]
        for i in range(GRAN_PER_BLOCK):
            @pl.when(i < nv)
            def _():
                if wait:
                    granule_out(blk, sl, i).wait()
                else:
                    granule_out(blk, sl, i).start()

    def steady_step(sl):
        wait_gather(sl)
        for i in range(GRAN_PER_BLOCK):
            granule_out(b - 2, sl, i).wait()
        start_gather(b + 1, 1 - sl)
        half = block_rows // 2
        for part in range(2):
            ffn(sl, part * half, (part + 1) * half)
            for i in range(part * GRAN_PER_BLOCK // 2, (part + 1) * GRAN_PER_BLOCK // 2):
                granule_out(b, sl, i).start()

    def general_step(sl):
        @pl.when(b == 0)
        def _():
            start_gather(b, sl)

        @pl.when(b + 1 < nb)
        def _():
            start_gather(b + 1, 1 - sl)

        wait_gather(sl)

        @pl.when(b >= 2)
        def _():
            scatter(b - 2, sl, True)

        ffn(sl, 0, block_rows)
        scatter(b, sl, False)

        @pl.when(b == nb - 1)
        def _():
            scatter(b, sl, True)

            @pl.when(b >= 1)
            def _():
                scatter(b - 1, 1 - sl, True)

    steady = (b >= 2) & (b + 1 < nb) & full(b) & full(jnp.maximum(b - 2, 0))
    for sl in range(2):
        @pl.when((b < nb) & (lax.rem(b, 2) == sl) & steady)
        def _():
            steady_step(sl)

        @pl.when((b < nb) & (lax.rem(b, 2) == sl) & jnp.logical_not(steady))
        def _():
            general_step(sl)


def _moe_ffn(xs, plan, w1, w3, w2, layer, nb_max):
    nrows, d = xs.shape
    f = w1.shape[-1]
    rows = GRAN_PER_BLOCK * GRAN
    wspec = lambda shape: pl.BlockSpec((1, 1) + shape, lambda b, nv, be, nb, src: (layer, be[b], 0, 0))
    ys = pl.pallas_call(
        _moe_ffn_kernel,
        out_shape=jax.ShapeDtypeStruct((nrows // GRAN, GRAN, d), BF16),
        grid_spec=pltpu.PrefetchScalarGridSpec(
            num_scalar_prefetch=4, grid=(nb_max,),
            in_specs=[pl.BlockSpec(memory_space=pl.ANY), wspec((d, f)), wspec((d, f)), wspec((f, d))],
            out_specs=pl.BlockSpec(memory_space=pl.ANY),
            scratch_shapes=[pltpu.VMEM((rows, d), BF16)] * 4
            + [pltpu.VMEM((d, f), BF16), pltpu.VMEM((d, f), BF16), pltpu.VMEM((f, d), BF16),
               pltpu.SemaphoreType.DMA((2,)), pltpu.SemaphoreType.DMA((2,))]),
        input_output_aliases={4: 0},
        compiler_params=pltpu.CompilerParams(dimension_semantics=("arbitrary",),
                                             vmem_limit_bytes=VMEM_LIMIT),
        name="moe_ffn",
    )(*plan, xs.reshape(nrows // GRAN, GRAN, d), w1, w3, w2)
    return ys.reshape(nrows, d)


def _final_kernel(xb_ref, ys_ref, posw_ref, mod_ref, nw_ref, o_ref):
    x = xb_ref[0] + mod_ref[0, 0][5:6] * _combine(ys_ref, posw_ref[0, 0])
    o_ref[0] = _rmsnorm(x, nw_ref[...])


def _final(xb, ys, posw, modx, nw, n_ctx_tiles):
    bsz, ntok, d = xb.shape
    nt_all = ntok // TILE
    nt = nt_all - n_ctx_tiles
    lat = lambda w: pl.BlockSpec((1, TILE, w), lambda b, j: (b, j + n_ctx_tiles, 0))
    return pl.pallas_call(
        _final_kernel,
        out_shape=jax.ShapeDtypeStruct((bsz, nt * TILE, d), F32),
        grid=(bsz, nt),
        in_specs=[lat(d), pl.BlockSpec((SORT_ROWS, d), lambda b, j: (b * nt_all + j + n_ctx_tiles, 0)),
                  pl.BlockSpec((1, 1, 16, TILE), lambda b, j: (b, j + n_ctx_tiles, 0, 0)),
                  pl.BlockSpec((1, 1, 6, d), lambda b, j: (b, 1, 0, 0)),
                  pl.BlockSpec(nw.shape, lambda b, j: (0, 0))],
        out_specs=pl.BlockSpec((1, TILE, d), lambda b, j: (b, j, 0)),
        compiler_params=pltpu.CompilerParams(dimension_semantics=("parallel", "parallel"),
                                             vmem_limit_bytes=VMEM_LIMIT),
        name="final_norm",
    )(xb, ys, posw, modx, nw)


def kernel(x, c, ctx, c_ctx, norm1_w, norm2_w, w_ada, b_ada, w_in, hgrn_lb_logits, hgrn_gnorm_w, w_pool,
           pool_scale, w_out, w_router, router_bias, w1_experts, w3_experts, w2_experts, w1_shared,
           w3_shared, w2_shared, final_norm_w):
    bsz, seq_len, d = x.shape
    ctx_len = ctx.shape[1]
    depth = w_ada.shape[0]
    assert ctx_len == TILE and seq_len % TILE == 0 and seq_len % GRID_W == 0
    ntok = ctx_len + seq_len
    ntile = bsz * ntok // TILE
    nb_max = -(-ntile * MAX_GRAN_PER_TILE // GRAN_PER_BLOCK) + N_EXPERTS

    lf, lbm = _level_matrices()
    lower, upper = _tri_matrices()
    pa, pinv = _pool_matrices()
    consts = (jnp.asarray(np.stack([lower, upper]), BF16), jnp.asarray(np.maximum(lf, lbm)),
              jnp.asarray(pa, BF16), jnp.asarray(pinv))

    p_lb = jax.nn.softmax(hgrn_lb_logits.astype(F32), axis=0)
    lb_all = jnp.cumsum(p_lb, axis=0) - p_lb[0:1]

    rows = 16
    cvec = jnp.concatenate([c, c_ctx[None, :], jnp.zeros((rows - bsz - 1, d), F32)], axis=0)
    mod = _ada_table(cvec, w_ada, b_ada)
    mod_lat = mod[:, :bsz].reshape(depth, bsz, 6, d)
    mod_ctx = jnp.broadcast_to(mod[:, bsz].reshape(depth, 1, 6, d), (depth, bsz, 6, d))
    modx = jnp.stack([mod_ctx, mod_lat], axis=2)

    xs = (ctx, x)
    for l in range(depth):
        has_prev = l > 0
        outs = _mixer_in(has_prev, xs + ((modx[l - 1],) if has_prev else ()), modx[l], norm1_w[l][None, :],
                         w_in[l].astype(BF16), lb_all[l], consts, w_pool[l].astype(BF16),
                         pool_scale[l][None, :])
        oi, qe, ut, a, og, po = outs[:6]
        x_cur = (outs[6],) if has_prev else xs
        sf, sb = _scan_state(ut, a)
        wr_t = w_router[l].T
        wr_hi = wr_t.astype(BF16)
        xb, rows_sorted, posw, cnt = _mixer_out(
            x_cur, oi, qe, sf, sb, og, po, modx[l], hgrn_gnorm_w[l][None, :], w_out[l].astype(BF16),
            norm2_w[l][None, :], jnp.stack([wr_hi, (wr_t - wr_hi.astype(F32)).astype(BF16)]),
            router_bias[l][:, None], w1_shared[l].astype(BF16),
            w3_shared[l].astype(BF16), w2_shared[l].astype(BF16), consts[0])
        plan = _moe_plan(cnt[:, :, :, 0].reshape(ntile, N_EXPERTS).astype(jnp.int32), nb_max)
        ys = _moe_ffn(rows_sorted, plan, w1_experts, w3_experts, w2_experts, l, nb_max)
        xs = (xb, ys, posw)
    return _final(xs[0], xs[1], xs[2], modx[depth - 1], final_norm_w[None, :], ctx_len // TILE)
```

```python
import functools

import numpy as np
import jax
import jax.numpy as jnp
from jax import lax
from jax.experimental import pallas as pl
from jax.experimental.pallas import tpu as pltpu

F32 = jnp.float32
BF16 = jnp.bfloat16

EPS = 1e-6
LB_FLOOR = 1e-30
MASK_SCORE = -1e4
GRID_W = 64
HEAD_DIM = 128
N_HEADS = 4
HGRN_W = N_HEADS * HEAD_DIM
POOL_WINDOWS = (2, 4, 8, 16)
POOL_GROUPS = len(POOL_WINDOWS)
POOL_GROUP_DIM = 128
POOL_W = POOL_GROUPS * POOL_GROUP_DIM
N_EXPERTS = 64
TOP_K = 6
N_EXPERT_GROUPS = 8
EXPERTS_PER_GROUP = N_EXPERTS // N_EXPERT_GROUPS
TOPK_GROUPS = 4
ROUTED_SCALE = 2.5

TILE = 256
N_LEVELS = 8
GRAN = 16
MAX_GRAN_PER_TILE = (TILE * TOP_K + N_EXPERTS * (GRAN - 1)) // GRAN
GRAN_PER_TILE = -(-MAX_GRAN_PER_TILE * GRAN // TILE) * TILE // GRAN
SORT_ROWS = GRAN_PER_TILE * GRAN
GRAN_PER_BLOCK = 64
VMEM_LIMIT = 56 * 1024 * 1024
NEG_INF = float("-inf")
LOG2E = 1.4426950408889634


def _sigmoid(z):
    return 1.0 / (1.0 + jnp.exp(-z))


def _silu(z):
    return z * _sigmoid(z)


def _dot(a, b):
    return jnp.dot(a, b, preferred_element_type=F32)


def _dot_nt(a, b):
    return lax.dot_general(a, b, (((1,), (1,)), ((), ())), preferred_element_type=F32)


def _dot_tn(a, b):
    return lax.dot_general(a, b, (((0,), (0,)), ((), ())), preferred_element_type=F32)


def _split3(g):
    hi = g.astype(BF16)
    r1 = g - hi.astype(F32)
    mid = r1.astype(BF16)
    lo = (r1 - mid.astype(F32)).astype(BF16)
    return hi, mid, lo


def _rmsnorm(x, w):
    return x * lax.rsqrt(jnp.mean(x * x, axis=-1, keepdims=True) + EPS) * w


def _level_matrices():
    t = np.arange(TILE)
    xor = t[:, None] ^ t[None, :]
    lvl = np.floor(np.log2(np.maximum(xor, 1))).astype(np.int32)
    lf = np.where(t[:, None] > t[None, :], lvl, -1).astype(np.int32)
    return lf, lf.T.copy()


def _tri_matrices():
    t = np.arange(TILE)
    lower = (t[:, None] >= t[None, :]).astype(np.float32)
    return lower, lower.T.copy()


def _pool_matrices():
    a = np.zeros((2, POOL_GROUPS, TILE, TILE), np.float32)
    inv = np.zeros((2, TILE, POOL_W), np.float32)
    t = np.arange(TILE)
    for kind, n in ((0, TILE), (1, GRID_W)):
        pos, base = t % n, (t // n) * n
        for gi, w in enumerate(POOL_WINDOWS):
            lo = np.clip(pos - w // 2, 0, n)
            hi = np.clip(pos + w - w // 2, 0, n)
            s = t[None, :]
            a[kind, gi] = ((s >= (base + lo)[:, None]) & (s < (base + hi)[:, None])).astype(np.float32)
            inv[kind, :, gi * POOL_GROUP_DIM:(gi + 1) * POOL_GROUP_DIM] = (1.0 / (hi - lo))[:, None]
    return a, inv


def _ada_kernel(c_ref, w_ref, b_ref, o_ref):
    o_ref[0] = jnp.dot(_silu(c_ref[...]), w_ref[0], preferred_element_type=F32,
                       precision=lax.Precision.HIGHEST) + b_ref[0]


def _ada_table(cvec, w_ada, b_ada):
    depth, d, n = w_ada.shape
    nb = 4
    rows = cvec.shape[0]
    return pl.pallas_call(
        _ada_kernel,
        out_shape=jax.ShapeDtypeStruct((depth, rows, n), F32),
        grid=(depth, nb),
        in_specs=[pl.BlockSpec((rows, d), lambda l, j: (0, 0)),
                  pl.BlockSpec((1, d, n // nb), lambda l, j: (l, 0, j)),
                  pl.BlockSpec((1, 1, n // nb), lambda l, j: (l, 0, j))],
        out_specs=pl.BlockSpec((1, rows, n // nb), lambda l, j: (l, 0, j)),
        compiler_params=pltpu.CompilerParams(dimension_semantics=("arbitrary", "arbitrary"),
                                             vmem_limit_bytes=VMEM_LIMIT),
        name="ada_table",
    )(cvec, w_ada, b_ada.reshape(depth, 1, n))


def _reference_rows(b, lvl, backward):
    w = 1 << lvl
    pm = w if backward else w - 1
    if w >= 4:
        n = TILE // (2 * w)
        row = b.reshape(n, 2 * w, HEAD_DIM)[:, pm:pm + 1, :]
        return jnp.broadcast_to(row, (n, 2 * w, HEAD_DIM)).reshape(TILE, HEAD_DIM)
    pos = lax.broadcasted_iota(jnp.int32, (TILE, HEAD_DIM), 0) & (2 * w - 1)
    out = b
    for p in range(2 * w):
        if p == pm:
            continue
        out = jnp.where(pos == p, pltpu.roll(b, (p - pm) % TILE, axis=0), out)
    return out


def _block_positions(posw, rb):
    rel = posw[0:8, :] - float(rb * TILE)
    rel = jnp.where((rel >= 0.0) & (rel < float(TILE)), rel, -1.0).astype(BF16)
    return [rel[k:k + 1, :] for k in range(TOP_K)]


def _combine(ys_ref, posw):
    row = lax.broadcasted_iota(jnp.int32, (TILE, TILE), 0).astype(F32).astype(BF16)
    wts = posw[8:16, :].astype(BF16)
    zero = jnp.zeros((TILE, TILE), BF16)
    acc = None
    for rb in range(SORT_ROWS // TILE):
        pw = zero
        for k, rel in enumerate(_block_positions(posw, rb)):
            pw = pw + jnp.where(row == rel, wts[k:k + 1, :], zero)
        part = _dot_tn(pw, ys_ref[rb * TILE:(rb + 1) * TILE, :])
        acc = part if acc is None else acc + part
    return acc


def _mixer_in_kernel(has_prev, *refs):
    if has_prev:
        xb_ref, ys_ref, posw_ref, modp_ref = refs[:4]
        refs = refs[4:]
    else:
        ctx_ref, x_ref = refs[:2]
        refs = refs[2:]
    (mod_ref, nw_ref, win_ref, lb_ref, tri_ref, lvl_ref, pa_ref, pinv_ref, wp_ref, ps_ref) = refs[:10]
    outs = refs[10:]
    if has_prev:
        oi_ref, qe_ref, ut_ref, a_ref, og_ref, po_ref, xo_ref = outs
        x = xb_ref[0] + modp_ref[0, 0][5:6] * _combine(ys_ref, posw_ref[0, 0])
        xo_ref[0] = x
    else:
        oi_ref, qe_ref, ut_ref, a_ref, og_ref, po_ref = outs
        x = _tile_of(ctx_ref, x_ref)

    m = mod_ref[0, 0]
    hx = (_rmsnorm(x, nw_ref[...]) * (1.0 + m[1:2]) + m[0:1]).astype(BF16)

    def proj(col):
        return _dot(hx, win_ref[:, col * HGRN_W:(col + 1) * HGRN_W])

    v = proj(2).astype(BF16)
    q = _silu(proj(3)) * (HEAD_DIM ** -0.5)
    og_ref[0] = _silu(proj(4)).astype(BF16)

    p = proj(5)
    pb = p.astype(BF16)
    pinv = pinv_ref[0]
    for g in range(POOL_GROUPS):
        sl = slice(g * POOL_GROUP_DIM, (g + 1) * POOL_GROUP_DIM)
        pooled = _dot(pa_ref[0, g], pb[:, sl]) * pinv[:, sl] - p[:, sl]
        po_ref[0, :, sl] = (_dot(pooled.astype(BF16), wp_ref[g]) * ps_ref[:, sl]).astype(BF16)

    ks, bs = [], []
    for d in range(2):
        z = proj(d)
        lb = lb_ref[d:d + 1, :]
        sg = _sigmoid(z)
        kk = (1.0 - lb) * (1.0 - sg)
        g = jnp.log(jnp.maximum(lb, LB_FLOOR) + (1.0 - lb) * sg)
        hi, mid, lo = _split3(g)
        tri = tri_ref[d]
        b = _dot(tri, hi) + _dot(tri, mid) + _dot(tri, lo)
        b_tot = b[TILE - 1:TILE, :] if d == 0 else b[0:1, :]
        qe_ref[0, :, d * HGRN_W:(d + 1) * HGRN_W] = (q * jnp.exp(b)).astype(BF16)
        k_end = (kk * jnp.exp(b_tot - b)).astype(BF16)
        a_tot = jnp.exp(b_tot)
        for h in range(N_HEADS):
            sl = slice(h * HEAD_DIM, (h + 1) * HEAD_DIM)
            ut_ref[0, 0, d, h] = _dot_tn(v[:, sl], k_end[:, sl])
            a_ref[0, 0, d, h] = a_tot[:, sl]
        ks.append(kk)
        bs.append(b)

    rows = lax.broadcasted_iota(jnp.int32, (TILE, HEAD_DIM), 0)
    upper_f32 = [((rows >> lvl) & 1) == 1 for lvl in range(N_LEVELS)]
    upper_b16 = [jnp.where(u, 1.0, 0.0).astype(BF16) > 0 for u in upper_f32]
    eye = (lax.broadcasted_iota(jnp.int32, (TILE, TILE), 0)
           == lax.broadcasted_iota(jnp.int32, (TILE, TILE), 1))
    lmat = lvl_ref[...]
    zero = jnp.zeros((TILE, HEAD_DIM), BF16)
    for h in range(N_HEADS):
        sl = slice(h * HEAD_DIM, (h + 1) * HEAD_DIM)
        qh = q[:, sl]
        qb, kf, kb = qh.astype(BF16), ks[0][:, sl].astype(BF16), ks[1][:, sl].astype(BF16)
        bf, bb = bs[0][:, sl] * LOG2E, bs[1][:, sl] * LOG2E
        diag = jnp.sum(qh * (ks[0][:, sl] + ks[1][:, sl]), axis=-1, keepdims=True)
        scores = jnp.where(eye, diag, 0.0)
        for lvl in range(N_LEVELS):
            up, upb = upper_f32[lvl], upper_b16[lvl]
            ref_f = _reference_rows(bf, lvl, False)
            ref_b = _reference_rows(bb, lvl, True)
            arg_f = jnp.where(up, bf - ref_f, ref_f - bf)
            arg_b = jnp.where(up, ref_b - bb, bb - ref_b)
            xf = jnp.where(upb, qb, kf) * jnp.exp2(arg_f.astype(BF16))
            xb = jnp.where(upb, kb, qb) * jnp.exp2(arg_b.astype(BF16))
            lhs = jnp.concatenate([jnp.where(upb, xf, zero), jnp.where(upb, zero, xb)], axis=-1)
            rhs = jnp.concatenate([xf, xb], axis=-1)
            scores = jnp.where(lmat == lvl, _dot_nt(lhs, rhs), scores)
        oi_ref[0, :, sl] = _dot(scores.astype(BF16), v[:, sl])


def _tile_of(ctx_ref, x_ref):
    return jnp.where(pl.program_id(1) == 0, ctx_ref[0], x_ref[0])


def _split_specs(d):
    return [pl.BlockSpec((1, TILE, d), lambda b, j: (b, 0, 0)),
            pl.BlockSpec((1, TILE, d), lambda b, j: (b, jnp.maximum(j - 1, 0), 0))]


def _mixer_in(has_prev, xs, modx, nw, win, lb, consts, wp, ps):
    tri, lvlm, pa, pinv = consts
    bsz, _, d = xs[0].shape
    ntok = xs[0].shape[1] if has_prev else xs[0].shape[1] + xs[1].shape[1]
    nt = ntok // TILE
    tok = lambda w: pl.BlockSpec((1, TILE, w), lambda b, j: (b, j, 0))
    modspec = pl.BlockSpec((1, 1, 6, d), lambda b, j: (b, jnp.minimum(j, 1), 0, 0))
    full = lambda a: pl.BlockSpec(a.shape, lambda b, j: (0,) * a.ndim)
    sorted_rows = pl.BlockSpec((SORT_ROWS, d), lambda b, j: (b * nt + j, 0))
    picks = pl.BlockSpec((1, 1, 16, TILE), lambda b, j: (b, j, 0, 0))
    in_specs = ([tok(d), sorted_rows, picks, modspec] if has_prev else _split_specs(d)) + [
        modspec, full(nw), full(win), full(lb), full(tri), full(lvlm),
        pl.BlockSpec((1,) + pa.shape[1:], lambda b, j: (jnp.minimum(j, 1), 0, 0, 0)),
        pl.BlockSpec((1,) + pinv.shape[1:], lambda b, j: (jnp.minimum(j, 1), 0, 0)),
        full(wp), full(ps)]
    out_shape = [jax.ShapeDtypeStruct((bsz, ntok, HGRN_W), F32),
                 jax.ShapeDtypeStruct((bsz, ntok, 2 * HGRN_W), BF16),
                 jax.ShapeDtypeStruct((bsz, nt, 2, N_HEADS, HEAD_DIM, HEAD_DIM), F32),
                 jax.ShapeDtypeStruct((bsz, nt, 2, N_HEADS, 1, HEAD_DIM), F32),
                 jax.ShapeDtypeStruct((bsz, ntok, HGRN_W), BF16),
                 jax.ShapeDtypeStruct((bsz, ntok, POOL_W), BF16)]
    out_specs = [tok(HGRN_W), tok(2 * HGRN_W),
                 pl.BlockSpec((1, 1, 2, N_HEADS, HEAD_DIM, HEAD_DIM), lambda b, j: (b, j, 0, 0, 0, 0)),
                 pl.BlockSpec((1, 1, 2, N_HEADS, 1, HEAD_DIM), lambda b, j: (b, j, 0, 0, 0, 0)),
                 tok(HGRN_W), tok(POOL_W)]
    if has_prev:
        out_shape.append(jax.ShapeDtypeStruct((bsz, ntok, d), F32))
        out_specs.append(tok(d))
    args = list(xs) + [modx, nw, win, lb, tri, lvlm, pa, pinv, wp, ps]
    return pl.pallas_call(
        functools.partial(_mixer_in_kernel, has_prev),
        out_shape=out_shape, grid=(bsz, nt), in_specs=in_specs, out_specs=out_specs,
        compiler_params=pltpu.CompilerParams(dimension_semantics=("parallel", "parallel"),
                                             vmem_limit_bytes=VMEM_LIMIT),
        name="mixer_in",
    )(*args)


def _scan_kernel(utf_ref, af_ref, utb_ref, ab_ref, sf_ref, sb_ref, accf, accb):
    @pl.when(pl.program_id(1) == 0)
    def _():
        accf[...] = jnp.zeros_like(accf)
        accb[...] = jnp.zeros_like(accb)

    sf_ref[0, 0] = accf[...].astype(BF16)
    sb_ref[0, 0] = accb[...].astype(BF16)
    accf[...] = af_ref[0, 0, 0] * accf[...] + utf_ref[0, 0, 0]
    accb[...] = ab_ref[0, 0, 0] * accb[...] + utb_ref[0, 0, 0]


def _scan_state(ut, a):
    bsz, nt = ut.shape[:2]
    bwd = lambda s: jnp.where(s == 0, 0, nt - s)
    st = (N_HEADS, HEAD_DIM, HEAD_DIM)
    ut_spec = lambda d, o: pl.BlockSpec((1, 1, 1) + st, lambda b, s: (b, o(s), d, 0, 0, 0))
    a_spec = lambda d, o: pl.BlockSpec((1, 1, 1, N_HEADS, 1, HEAD_DIM), lambda b, s: (b, o(s), d, 0, 0, 0))
    ident = lambda s: s
    return pl.pallas_call(
        _scan_kernel,
        out_shape=[jax.ShapeDtypeStruct((bsz, nt) + st, BF16)] * 2,
        grid=(bsz, nt),
        in_specs=[ut_spec(0, ident), a_spec(0, ident), ut_spec(1, bwd), a_spec(1, bwd)],
        out_specs=[pl.BlockSpec((1, 1) + st, lambda b, s: (b, s, 0, 0, 0)),
                   pl.BlockSpec((1, 1) + st, lambda b, s: (b, bwd(s), 0, 0, 0))],
        scratch_shapes=[pltpu.VMEM(st, F32), pltpu.VMEM(st, F32)],
        compiler_params=pltpu.CompilerParams(dimension_semantics=("parallel", "arbitrary"),
                                             vmem_limit_bytes=VMEM_LIMIT),
        name="scan_state",
    )(ut, a, ut, a)


def _route(scores, bias):
    e, t = scores.shape
    ng, per = N_EXPERT_GROUPS, EXPERTS_PER_GROUP
    biased = scores + bias
    grouped = biased.reshape(ng, per, t)
    member = lax.broadcasted_iota(jnp.int32, (ng, per, t), 1).astype(F32)
    m1 = jnp.max(grouped, axis=1, keepdims=True)
    i1 = jnp.min(jnp.where(grouped == m1, member, float(per)), axis=1, keepdims=True)
    m2 = jnp.max(jnp.where(member == i1, NEG_INF, grouped), axis=1, keepdims=True)
    work = jnp.broadcast_to(m1 + m2, (ng, per, t)).reshape(e, t)
    eidx = lax.broadcasted_iota(jnp.int32, (e, t), 0)
    gidx = (eidx >> 3).astype(F32)
    eidx = eidx.astype(F32)

    def first_max(x, idx):
        mx = jnp.max(x, axis=0, keepdims=True)
        return jnp.min(jnp.where(x == mx, idx, float(4 * e)), axis=0, keepdims=True)

    keep = jnp.zeros_like(biased)
    for _ in range(TOPK_GROUPS):
        hit = gidx == first_max(work, gidx)
        keep = jnp.where(hit, 1.0, keep)
        work = jnp.where(hit, NEG_INF, work)
    work = jnp.where(keep > 0.0, biased, MASK_SCORE)
    sel = jnp.zeros_like(biased)
    hits = []
    for _ in range(TOP_K):
        hit = eidx == first_max(work, eidx)
        hits.append(hit)
        sel = jnp.where(hit, 1.0, sel)
        work = jnp.where(hit, NEG_INF, work)
    gate = jnp.where(sel > 0.0, scores, 0.0)
    return gate / jnp.sum(gate, axis=0, keepdims=True) * ROUTED_SCALE, sel, hits


def _sort_rows(gates, sel, hits, h2b, tri_upper, xs_ref, posw_ref, cnt_ref):
    e, t = sel.shape
    selb = sel.astype(BF16)
    incl = _dot(selb, tri_upper)
    cnt = _dot(selb, jnp.ones((t, 128), BF16))
    ngran = jnp.floor((cnt + float(GRAN - 1)) * (1.0 / GRAN))
    ei = lax.broadcasted_iota(jnp.int32, (e, e), 0)
    ej = lax.broadcasted_iota(jnp.int32, (e, e), 1)
    earlier = jnp.where(ej < ei, 1.0, 0.0).astype(BF16)
    run_start = _dot(earlier, ngran.astype(BF16)) * float(GRAN)
    slot = run_start[:, 0:1] + incl - sel
    cnt_ref[0, 0] = cnt

    ridx = lax.broadcasted_iota(jnp.int32, (16, t), 0)
    posw = jnp.where(ridx < 8, -1.0, 0.0)
    for k, hit in enumerate(hits):
        pos_k = jnp.sum(jnp.where(hit, slot, 0.0), axis=0, keepdims=True)
        w_k = jnp.sum(jnp.where(hit, gates, 0.0), axis=0, keepdims=True)
        posw = jnp.where(ridx == k, pos_k, jnp.where(ridx == 8 + k, w_k, posw))
    posw_ref[0, 0] = posw

    row = lax.broadcasted_iota(jnp.int32, (TILE, t), 0).astype(F32).astype(BF16)
    one, zero = jnp.ones((TILE, t), BF16), jnp.zeros((TILE, t), BF16)
    for rb in range(SORT_ROWS // TILE):
        onehot = zero
        for rel in _block_positions(posw, rb):
            onehot = onehot + jnp.where(row == rel, one, zero)
        xs_ref[rb * TILE:(rb + 1) * TILE, :] = _dot(onehot, h2b).astype(BF16)


def _mixer_out_kernel(split, *refs):
    x_in = _tile_of(refs[0], refs[1]) if split else refs[0][0]
    (oi_ref, qe_ref, sf_ref, sb_ref, og_ref, po_ref, mod_ref, gw_ref, wo_ref, nw2_ref, wr_ref, rb_ref,
     sw1_ref, sw3_ref, sw2_ref, tri_ref, xb_ref, xs_ref, posw_ref, cnt_ref) = refs[2 if split else 1:]
    m = mod_ref[0, 0]
    parts = []
    for h in range(N_HEADS):
        sl = slice(h * HEAD_DIM, (h + 1) * HEAD_DIM)
        o = (oi_ref[0, :, sl]
             + _dot_nt(qe_ref[0, :, sl], sf_ref[0, 0, h])
             + _dot_nt(qe_ref[0, :, HGRN_W + h * HEAD_DIM:HGRN_W + (h + 1) * HEAD_DIM],
                       sb_ref[0, 0, h]))
        y = _rmsnorm(o, gw_ref[...]) * og_ref[0, :, sl].astype(F32)
        parts.append(y.astype(BF16))
    mix = jnp.concatenate(parts + [po_ref[0]], axis=-1)
    x1 = x_in + m[2:3] * _dot(mix, wo_ref[...])
    h2 = _rmsnorm(x1, nw2_ref[...]) * (1.0 + m[4:5]) + m[3:4]
    h2b = h2.astype(BF16)
    shared = _dot((_silu(_dot(h2b, sw1_ref[...])) * _dot(h2b, sw3_ref[...])).astype(BF16), sw2_ref[...])
    xb_ref[0] = x1 + m[5:6] * shared
    h2lo = (h2 - h2b.astype(F32)).astype(BF16)
    logits = _dot_nt(wr_ref[0], h2b) + _dot_nt(wr_ref[0], h2lo) + _dot_nt(wr_ref[1], h2b)
    gates, sel, hits = _route(_sigmoid(logits), rb_ref[...])
    _sort_rows(gates, sel, hits, h2b, tri_ref[1], xs_ref, posw_ref, cnt_ref)


def _mixer_out(x_parts, oi, qe, sf, sb, og, po, modx, gw, wo, nw2, wr, rb, sw1, sw3, sw2, tri):
    split = len(x_parts) == 2
    bsz, ntok, _ = oi.shape
    d = x_parts[0].shape[-1]
    nt = ntok // TILE
    tok = lambda w: pl.BlockSpec((1, TILE, w), lambda b, j: (b, j, 0))
    st = pl.BlockSpec((1, 1, N_HEADS, HEAD_DIM, HEAD_DIM), lambda b, j: (b, j, 0, 0, 0))
    full = lambda a: pl.BlockSpec(a.shape, lambda b, j: (0,) * a.ndim)
    return pl.pallas_call(
        functools.partial(_mixer_out_kernel, split),
        out_shape=[jax.ShapeDtypeStruct((bsz, ntok, d), F32),
                   jax.ShapeDtypeStruct((bsz * nt * SORT_ROWS, d), BF16),
                   jax.ShapeDtypeStruct((bsz, nt, 16, TILE), F32),
                   jax.ShapeDtypeStruct((bsz, nt, N_EXPERTS, 128), F32)],
        grid=(bsz, nt),
        in_specs=(_split_specs(d) if split else [tok(d)]) + [
                  tok(HGRN_W), tok(2 * HGRN_W), st, st, tok(HGRN_W), tok(POOL_W),
                  pl.BlockSpec((1, 1, 6, d), lambda b, j: (b, jnp.minimum(j, 1), 0, 0)),
                  full(gw), full(wo), full(nw2), full(wr), full(rb), full(sw1), full(sw3), full(sw2),
                  full(tri)],
        out_specs=[tok(d), pl.BlockSpec((SORT_ROWS, d), lambda b, j: (b * nt + j, 0)),
                   pl.BlockSpec((1, 1, 16, TILE), lambda b, j: (b, j, 0, 0)),
                   pl.BlockSpec((1, 1, N_EXPERTS, 128), lambda b, j: (b, j, 0, 0))],
        compiler_params=pltpu.CompilerParams(dimension_semantics=("parallel", "parallel"),
                                             vmem_limit_bytes=VMEM_LIMIT),
        name="mixer_out",
    )(*x_parts, oi, qe, sf, sb, og, po, modx, gw, wo, nw2, wr, rb, sw1, sw3, sw2, tri)


def _moe_plan(cnt, nb_max):
    ntile, ne = cnt.shape
    ngran = (cnt + GRAN - 1) // GRAN
    run_start = jnp.cumsum(ngran, axis=1) - ngran
    ngran_t = ngran.T
    per_e = ngran_t.sum(axis=1)
    nblk = (per_e + GRAN_PER_BLOCK - 1) // GRAN_PER_BLOCK
    blk_end = jnp.cumsum(nblk)
    blk_start = blk_end - nblk
    nb = blk_end[-1]
    bidx = jnp.arange(nb_max, dtype=jnp.int32)
    be = jnp.sum((blk_end[None, :] <= jnp.minimum(bidx, nb - 1)[:, None]).astype(jnp.int32), axis=1)
    q0 = (bidx - blk_start[be]) * GRAN_PER_BLOCK
    nvalid = jnp.where(bidx < nb, jnp.clip(per_e[be] - q0, 0, GRAN_PER_BLOCK), 0)
    run_hi = jnp.cumsum(ngran_t, axis=1)
    run_lo = run_hi - ngran_t
    base = jnp.arange(ntile, dtype=jnp.int32)[None, :] * GRAN_PER_TILE + run_start.T - run_lo
    q = q0[:, None] + jnp.arange(GRAN_PER_BLOCK, dtype=jnp.int32)[None, :]
    tile = jnp.sum((run_hi[be][:, None, :] <= q[:, :, None]).astype(jnp.int32), axis=-1)
    src = jnp.take_along_axis(base[be], jnp.minimum(tile, ntile - 1), axis=1) + q
    src = jnp.where(tile < ntile, src, src[:, 0:1])
    return (nvalid.astype(jnp.int32), be, nb.reshape(1).astype(jnp.int32),
            src.reshape(-1).astype(jnp.int32))


def _moe_ffn_kernel(nv_ref, be_ref, nb_ref, src_ref, xs_hbm, w1_ref, w3_ref, w2_ref, ys_hbm,
                    xin0, xin1, yout0, yout1, w1b, w3b, w2b, sem_in, sem_out):
    b = pl.program_id(0)
    nb = nb_ref[0]
    xin, yout = (xin0, xin1), (yout0, yout1)
    block_rows = GRAN_PER_BLOCK * GRAN
    full = lambda blk: nv_ref[blk] == GRAN_PER_BLOCK

    @pl.when((b == 0) | (be_ref[b] != be_ref[jnp.maximum(b - 1, 0)]))
    def _():
        w1b[...] = w1_ref[0, 0].astype(BF16)
        w3b[...] = w3_ref[0, 0].astype(BF16)
        w2b[...] = w2_ref[0, 0].astype(BF16)

    def start_gather(blk, sl):
        for i in range(GRAN_PER_BLOCK):
            pltpu.make_async_copy(xs_hbm.at[src_ref[blk * GRAN_PER_BLOCK + i]],
                                  xin[sl].at[pl.ds(i * GRAN, GRAN), :], sem_in.at[sl]).start()

    def wait_gather(sl):
        for i in range(GRAN_PER_BLOCK):
            pltpu.make_async_copy(xs_hbm.at[0], xin[sl].at[pl.ds(i * GRAN, GRAN), :], sem_in.at[sl]).wait()

    def granule_out(blk, sl, i):
        return pltpu.make_async_copy(yout[sl].at[pl.ds(i * GRAN, GRAN), :],
                                     ys_hbm.at[src_ref[blk * GRAN_PER_BLOCK + i]], sem_out.at[sl])

    def ffn(sl, lo, hi):
        x = xin[sl][lo:hi, :]
        hid = (_silu(_dot(x, w1b[...])) * _dot(x, w3b[...])).astype(BF16)
        yout[sl][lo:hi, :] = _dot(hid, w2b[...]).astype(BF16)

    def scatter(blk, sl, wait):
        nv = nv_ref[blk]
        for i in range(GRAN_PER_BLOCK):
            @pl.when(i < nv)
            def _():
                if wait:
                    granule_out(blk, sl, i).wait()
                else:
                    granule_out(blk, sl, i).start()

    def steady_step(sl):
        wait_gather(sl)

        @pl.when(full(b - 2))
        def _():
            for i in range(GRAN_PER_BLOCK):
                granule_out(b - 2, sl, i).wait()

        @pl.when(jnp.logical_not(full(b - 2)))
        def _():
            scatter(b - 2, sl, True)

        start_gather(b + 1, 1 - sl)
        half = block_rows // 2
        for part in range(2):
            ffn(sl, part * half, (part + 1) * half)
            for i in range(part * GRAN_PER_BLOCK // 2, (part + 1) * GRAN_PER_BLOCK // 2):
                granule_out(b, sl, i).start()

    def general_step(sl):
        @pl.when(b == 0)
        def _():
            start_gather(b, sl)

        @pl.when(b + 1 < nb)
        def _():
            start_gather(b + 1, 1 - sl)

        wait_gather(sl)

        @pl.when(b >= 2)
        def _():
            scatter(b - 2, sl, True)

        ffn(sl, 0, block_rows)
        scatter(b, sl, False)

        @pl.when(b == nb - 1)
        def _():
            scatter(b, sl, True)

            @pl.when(b >= 1)
            def _():
                scatter(b - 1, 1 - sl, True)

    steady = (b >= 2) & (b + 1 < nb) & full(b)
    for sl in range(2):
        @pl.when((b < nb) & (lax.rem(b, 2) == sl) & steady)
        def _():
            steady_step(sl)

        @pl.when((b < nb) & (lax.rem(b, 2) == sl) & jnp.logical_not(steady))
        def _():
            general_step(sl)


def _moe_ffn(xs, plan, w1, w3, w2, layer, nb_max):
    nrows, d = xs.shape
    f = w1.shape[-1]
    rows = GRAN_PER_BLOCK * GRAN
    wspec = lambda shape: pl.BlockSpec((1, 1) + shape, lambda b, nv, be, nb, src: (layer, be[b], 0, 0))
    ys = pl.pallas_call(
        _moe_ffn_kernel,
        out_shape=jax.ShapeDtypeStruct((nrows // GRAN, GRAN, d), BF16),
        grid_spec=pltpu.PrefetchScalarGridSpec(
            num_scalar_prefetch=4, grid=(nb_max,),
            in_specs=[pl.BlockSpec(memory_space=pl.ANY), wspec((d, f)), wspec((d, f)), wspec((f, d))],
            out_specs=pl.BlockSpec(memory_space=pl.ANY),
            scratch_shapes=[pltpu.VMEM((rows, d), BF16)] * 4
            + [pltpu.VMEM((d, f), BF16), pltpu.VMEM((d, f), BF16), pltpu.VMEM((f, d), BF16),
               pltpu.SemaphoreType.DMA((2,)), pltpu.SemaphoreType.DMA((2,))]),
        input_output_aliases={4: 0},
        compiler_params=pltpu.CompilerParams(dimension_semantics=("arbitrary",),
                                             vmem_limit_bytes=VMEM_LIMIT),
        name="moe_ffn",
    )(*plan, xs.reshape(nrows // GRAN, GRAN, d), w1, w3, w2)
    return ys.reshape(nrows, d)


def _final_kernel(xb_ref, ys_ref, posw_ref, mod_ref, nw_ref, o_ref):
    x = xb_ref[0] + mod_ref[0, 0][5:6] * _combine(ys_ref, posw_ref[0, 0])
    o_ref[0] = _rmsnorm(x, nw_ref[...])


def _final(xb, ys, posw, modx, nw, n_ctx_tiles):
    bsz, ntok, d = xb.shape
    nt_all = ntok // TILE
    nt = nt_all - n_ctx_tiles
    lat = lambda w: pl.BlockSpec((1, TILE, w), lambda b, j: (b, j + n_ctx_tiles, 0))
    return pl.pallas_call(
        _final_kernel,
        out_shape=jax.ShapeDtypeStruct((bsz, nt * TILE, d), F32),
        grid=(bsz, nt),
        in_specs=[lat(d), pl.BlockSpec((SORT_ROWS, d), lambda b, j: (b * nt_all + j + n_ctx_tiles, 0)),
                  pl.BlockSpec((1, 1, 16, TILE), lambda b, j: (b, j + n_ctx_tiles, 0, 0)),
                  pl.BlockSpec((1, 1, 6, d), lambda b, j: (b, 1, 0, 0)),
                  pl.BlockSpec(nw.shape, lambda b, j: (0, 0))],
        out_specs=pl.BlockSpec((1, TILE, d), lambda b, j: (b, j, 0)),
        compiler_params=pltpu.CompilerParams(dimension_semantics=("parallel", "parallel"),
                                             vmem_limit_bytes=VMEM_LIMIT),
        name="final_norm",
    )(xb, ys, posw, modx, nw)


def kernel(x, c, ctx, c_ctx, norm1_w, norm2_w, w_ada, b_ada, w_in, hgrn_lb_logits, hgrn_gnorm_w, w_pool,
           pool_scale, w_out, w_router, router_bias, w1_experts, w3_experts, w2_experts, w1_shared,
           w3_shared, w2_shared, final_norm_w):
    bsz, seq_len, d = x.shape
    ctx_len = ctx.shape[1]
    depth = w_ada.shape[0]
    assert ctx_len == TILE and seq_len % TILE == 0 and seq_len % GRID_W == 0
    ntok = ctx_len + seq_len
    ntile = bsz * ntok // TILE
    nb_max = -(-ntile * MAX_GRAN_PER_TILE // GRAN_PER_BLOCK) + N_EXPERTS

    lf, lbm = _level_matrices()
    lower, upper = _tri_matrices()
    pa, pinv = _pool_matrices()
    consts = (jnp.asarray(np.stack([lower, upper]), BF16), jnp.asarray(np.maximum(lf, lbm)),
              jnp.asarray(pa, BF16), jnp.asarray(pinv))

    p_lb = jax.nn.softmax(hgrn_lb_logits.astype(F32), axis=0)
    lb_all = jnp.cumsum(p_lb, axis=0) - p_lb[0:1]

    rows = 16
    cvec = jnp.concatenate([c, c_ctx[None, :], jnp.zeros((rows - bsz - 1, d), F32)], axis=0)
    mod = _ada_table(cvec, w_ada, b_ada)
    mod_lat = mod[:, :bsz].reshape(depth, bsz, 6, d)
    mod_ctx = jnp.broadcast_to(mod[:, bsz].reshape(depth, 1, 6, d), (depth, bsz, 6, d))
    modx = jnp.stack([mod_ctx, mod_lat], axis=2)

    xs = (ctx, x)
    for l in range(depth):
        has_prev = l > 0
        outs = _mixer_in(has_prev, xs + ((modx[l - 1],) if has_prev else ()), modx[l], norm1_w[l][None, :],
                         w_in[l].astype(BF16), lb_all[l], consts, w_pool[l].astype(BF16),
                         pool_scale[l][None, :])
        oi, qe, ut, a, og, po = outs[:6]
        x_cur = (outs[6],) if has_prev else xs
        sf, sb = _scan_state(ut, a)
        wr_t = w_router[l].T
        wr_hi = wr_t.astype(BF16)
        xb, rows_sorted, posw, cnt = _mixer_out(
            x_cur, oi, qe, sf, sb, og, po, modx[l], hgrn_gnorm_w[l][None, :], w_out[l].astype(BF16),
            norm2_w[l][None, :], jnp.stack([wr_hi, (wr_t - wr_hi.astype(F32)).astype(BF16)]),
            router_bias[l][:, None], w1_shared[l].astype(BF16),
            w3_shared[l].astype(BF16), w2_shared[l].astype(BF16), consts[0])
        plan = _moe_plan(cnt[:, :, :, 0].reshape(ntile, N_EXPERTS).astype(jnp.int32), nb_max)
        ys = _moe_ffn(rows_sorted, plan, w1_experts, w3_experts, w2_experts, l, nb_max)
        xs = (xb, ys, posw)
    return _final(xs[0], xs[1], xs[2], modx[depth - 1], final_norm_w[None, :], ctx_len // TILE)
```

```python
import functools

import numpy as np
import jax
import jax.numpy as jnp
from jax import lax
from jax.experimental import pallas as pl
from jax.experimental.pallas import tpu as pltpu

F32 = jnp.float32
BF16 = jnp.bfloat16

EPS = 1e-6
LB_FLOOR = 1e-30
MASK_SCORE = -1e4
GRID_W = 64
HEAD_DIM = 128
N_HEADS = 4
HGRN_W = N_HEADS * HEAD_DIM
POOL_WINDOWS = (2, 4, 8, 16)
POOL_GROUPS = len(POOL_WINDOWS)
POOL_GROUP_DIM = 128
POOL_W = POOL_GROUPS * POOL_GROUP_DIM
N_EXPERTS = 64
TOP_K = 6
N_EXPERT_GROUPS = 8
EXPERTS_PER_GROUP = N_EXPERTS // N_EXPERT_GROUPS
TOPK_GROUPS = 4
ROUTED_SCALE = 2.5

TILE = 256
N_LEVELS = 8
GRAN = 16
MAX_GRAN_PER_TILE = (TILE * TOP_K + N_EXPERTS * (GRAN - 1)) // GRAN
GRAN_PER_TILE = -(-MAX_GRAN_PER_TILE * GRAN // TILE) * TILE // GRAN
SORT_ROWS = GRAN_PER_TILE * GRAN
ALWAYS_USED_BLOCKS = TILE * TOP_K // TILE + 2
GRAN_PER_BLOCK = 64
VMEM_LIMIT = 56 * 1024 * 1024
NEG_INF = float("-inf")
LOG2E = 1.4426950408889634


def _sigmoid(z):
    return 1.0 / (1.0 + jnp.exp(-z))


def _silu(z):
    return z * _sigmoid(z)


def _dot(a, b):
    return jnp.dot(a, b, preferred_element_type=F32)


def _dot_nt(a, b):
    return lax.dot_general(a, b, (((1,), (1,)), ((), ())), preferred_element_type=F32)


def _dot_tn(a, b):
    return lax.dot_general(a, b, (((0,), (0,)), ((), ())), preferred_element_type=F32)


def _split3(g):
    hi = g.astype(BF16)
    r1 = g - hi.astype(F32)
    mid = r1.astype(BF16)
    lo = (r1 - mid.astype(F32)).astype(BF16)
    return hi, mid, lo


def _rmsnorm(x, w):
    return x * lax.rsqrt(jnp.mean(x * x, axis=-1, keepdims=True) + EPS) * w


def _level_matrices():
    t = np.arange(TILE)
    xor = t[:, None] ^ t[None, :]
    lvl = np.floor(np.log2(np.maximum(xor, 1))).astype(np.int32)
    lf = np.where(t[:, None] > t[None, :], lvl, -1).astype(np.int32)
    return lf, lf.T.copy()


def _tri_matrices():
    t = np.arange(TILE)
    lower = (t[:, None] >= t[None, :]).astype(np.float32)
    return lower, lower.T.copy()


def _pool_matrices():
    a = np.zeros((2, POOL_GROUPS, TILE, TILE), np.float32)
    inv = np.zeros((2, TILE, POOL_W), np.float32)
    t = np.arange(TILE)
    for kind, n in ((0, TILE), (1, GRID_W)):
        pos, base = t % n, (t // n) * n
        for gi, w in enumerate(POOL_WINDOWS):
            lo = np.clip(pos - w // 2, 0, n)
            hi = np.clip(pos + w - w // 2, 0, n)
            s = t[None, :]
            a[kind, gi] = ((s >= (base + lo)[:, None]) & (s < (base + hi)[:, None])).astype(np.float32)
            inv[kind, :, gi * POOL_GROUP_DIM:(gi + 1) * POOL_GROUP_DIM] = (1.0 / (hi - lo))[:, None]
    return a, inv


def _ada_kernel(c_ref, w_ref, b_ref, o_ref):
    o_ref[0] = jnp.dot(_silu(c_ref[...]), w_ref[0], preferred_element_type=F32,
                       precision=lax.Precision.HIGHEST) + b_ref[0]


def _ada_table(cvec, w_ada, b_ada):
    depth, d, n = w_ada.shape
    nb = 4
    rows = cvec.shape[0]
    return pl.pallas_call(
        _ada_kernel,
        out_shape=jax.ShapeDtypeStruct((depth, rows, n), F32),
        grid=(depth, nb),
        in_specs=[pl.BlockSpec((rows, d), lambda l, j: (0, 0)),
                  pl.BlockSpec((1, d, n // nb), lambda l, j: (l, 0, j)),
                  pl.BlockSpec((1, 1, n // nb), lambda l, j: (l, 0, j))],
        out_specs=pl.BlockSpec((1, rows, n // nb), lambda l, j: (l, 0, j)),
        compiler_params=pltpu.CompilerParams(dimension_semantics=("arbitrary", "arbitrary"),
                                             vmem_limit_bytes=VMEM_LIMIT),
        name="ada_table",
    )(cvec, w_ada, b_ada.reshape(depth, 1, n))


def _reference_rows(b, lvl, backward):
    w = 1 << lvl
    pm = w if backward else w - 1
    if w >= 4:
        n = TILE // (2 * w)
        row = b.reshape(n, 2 * w, HEAD_DIM)[:, pm:pm + 1, :]
        return jnp.broadcast_to(row, (n, 2 * w, HEAD_DIM)).reshape(TILE, HEAD_DIM)
    pos = lax.broadcasted_iota(jnp.int32, (TILE, HEAD_DIM), 0) & (2 * w - 1)
    out = b
    for p in range(2 * w):
        if p == pm:
            continue
        out = jnp.where(pos == p, pltpu.roll(b, (p - pm) % TILE, axis=0), out)
    return out


def _block_positions(posw, rb):
    rel = posw[0:8, :] - float(rb * TILE)
    rel = jnp.where((rel >= 0.0) & (rel < float(TILE)), rel, -1.0).astype(BF16)
    return [rel[k:k + 1, :] for k in range(TOP_K)]


def _combine(ys_ref, posw, acc_ref):
    row = lax.broadcasted_iota(jnp.int32, (TILE, TILE), 0).astype(F32).astype(BF16)
    wts = posw[8:16, :].astype(BF16)
    zero = jnp.zeros((TILE, TILE), BF16)
    last_pos = jnp.max(posw[0:8, :])

    def block(rb):
        pw = zero
        for k, rel in enumerate(_block_positions(posw, rb)):
            pw = pw + jnp.where(row == rel, wts[k:k + 1, :], zero)
        return _dot_tn(pw, ys_ref[rb * TILE:(rb + 1) * TILE, :])

    acc = block(0)
    for rb in range(1, ALWAYS_USED_BLOCKS):
        acc = acc + block(rb)
    acc_ref[...] = acc
    for rb in range(ALWAYS_USED_BLOCKS, SORT_ROWS // TILE):
        @pl.when(last_pos >= float(rb * TILE))
        def _():
            acc_ref[...] += block(rb)
    return acc_ref[...]


def _mixer_in_kernel(has_prev, *refs):
    if has_prev:
        xb_ref, ys_ref, posw_ref, modp_ref = refs[:4]
        refs = refs[4:]
    else:
        ctx_ref, x_ref = refs[:2]
        refs = refs[2:]
    (mod_ref, nw_ref, win_ref, lb_ref, tri_ref, lvl_ref, pa_ref, pinv_ref, wp_ref, ps_ref) = refs[:10]
    outs = refs[10:]
    if has_prev:
        oi_ref, qe_ref, ut_ref, a_ref, og_ref, po_ref, xo_ref, acc_ref = outs
        x = xb_ref[0] + modp_ref[0, 0][5:6] * _combine(ys_ref, posw_ref[0, 0], acc_ref)
        xo_ref[0] = x
    else:
        oi_ref, qe_ref, ut_ref, a_ref, og_ref, po_ref = outs
        x = _tile_of(ctx_ref, x_ref)

    m = mod_ref[0, 0]
    hx = (_rmsnorm(x, nw_ref[...]) * (1.0 + m[1:2]) + m[0:1]).astype(BF16)

    def proj(col):
        return _dot(hx, win_ref[:, col * HGRN_W:(col + 1) * HGRN_W])

    v = proj(2).astype(BF16)
    q = _silu(proj(3)) * (HEAD_DIM ** -0.5)
    og_ref[0] = _silu(proj(4)).astype(BF16)

    p = proj(5)
    pb = p.astype(BF16)
    pinv = pinv_ref[0]
    for g in range(POOL_GROUPS):
        sl = slice(g * POOL_GROUP_DIM, (g + 1) * POOL_GROUP_DIM)
        pooled = _dot(pa_ref[0, g], pb[:, sl]) * pinv[:, sl] - p[:, sl]
        po_ref[0, :, sl] = (_dot(pooled.astype(BF16), wp_ref[g]) * ps_ref[:, sl]).astype(BF16)

    ks, bs = [], []
    for d in range(2):
        z = proj(d)
        lb = lb_ref[d:d + 1, :]
        sg = _sigmoid(z)
        kk = (1.0 - lb) * (1.0 - sg)
        g = jnp.log(jnp.maximum(lb, LB_FLOOR) + (1.0 - lb) * sg)
        hi, mid, lo = _split3(g)
        tri = tri_ref[d]
        b = _dot(tri, hi) + _dot(tri, mid) + _dot(tri, lo)
        b_tot = b[TILE - 1:TILE, :] if d == 0 else b[0:1, :]
        qe_ref[0, :, d * HGRN_W:(d + 1) * HGRN_W] = (q * jnp.exp(b)).astype(BF16)
        k_end = (kk * jnp.exp(b_tot - b)).astype(BF16)
        a_tot = jnp.exp(b_tot)
        for h in range(N_HEADS):
            sl = slice(h * HEAD_DIM, (h + 1) * HEAD_DIM)
            ut_ref[0, 0, d, h] = _dot_tn(v[:, sl], k_end[:, sl])
            a_ref[0, 0, d, h] = a_tot[:, sl]
        ks.append(kk)
        bs.append(b)

    rows = lax.broadcasted_iota(jnp.int32, (TILE, HEAD_DIM), 0)
    upper_b16 = [(((rows >> lvl) & 1).astype(F32)).astype(BF16) > 0 for lvl in range(N_LEVELS)]
    eye = (lax.broadcasted_iota(jnp.int32, (TILE, TILE), 0)
           == lax.broadcasted_iota(jnp.int32, (TILE, TILE), 1))
    lmat = lvl_ref[...]
    zero = jnp.zeros((TILE, HEAD_DIM), BF16)
    for h in range(N_HEADS):
        sl = slice(h * HEAD_DIM, (h + 1) * HEAD_DIM)
        qh = q[:, sl]
        qb, kf, kb = qh.astype(BF16), ks[0][:, sl].astype(BF16), ks[1][:, sl].astype(BF16)
        bf, bb = bs[0][:, sl] * LOG2E, bs[1][:, sl] * LOG2E
        diag = jnp.sum(qh * (ks[0][:, sl] + ks[1][:, sl]), axis=-1, keepdims=True)
        scores = jnp.where(eye, diag, 0.0)
        for lvl in range(N_LEVELS):
            upb = upper_b16[lvl]
            arg_f = -jnp.abs(bf - _reference_rows(bf, lvl, False))
            arg_b = -jnp.abs(bb - _reference_rows(bb, lvl, True))
            xf = jnp.where(upb, qb, kf) * jnp.exp2(arg_f.astype(BF16))
            xb = jnp.where(upb, kb, qb) * jnp.exp2(arg_b.astype(BF16))
            lhs = jnp.concatenate([jnp.where(upb, xf, zero), jnp.where(upb, zero, xb)], axis=-1)
            rhs = jnp.concatenate([xf, xb], axis=-1)
            scores = jnp.where(lmat == lvl, _dot_nt(lhs, rhs), scores)
        oi_ref[0, :, sl] = _dot(scores.astype(BF16), v[:, sl])


def _tile_of(ctx_ref, x_ref):
    return jnp.where(pl.program_id(1) == 0, ctx_ref[0], x_ref[0])


def _split_specs(d):
    return [pl.BlockSpec((1, TILE, d), lambda b, j: (b, 0, 0)),
            pl.BlockSpec((1, TILE, d), lambda b, j: (b, jnp.maximum(j - 1, 0), 0))]


def _mixer_in(has_prev, xs, modx, nw, win, lb, consts, wp, ps):
    tri, lvlm, pa, pinv = consts
    bsz, _, d = xs[0].shape
    ntok = xs[0].shape[1] if has_prev else xs[0].shape[1] + xs[1].shape[1]
    nt = ntok // TILE
    tok = lambda w: pl.BlockSpec((1, TILE, w), lambda b, j: (b, j, 0))
    modspec = pl.BlockSpec((1, 1, 6, d), lambda b, j: (b, jnp.minimum(j, 1), 0, 0))
    full = lambda a: pl.BlockSpec(a.shape, lambda b, j: (0,) * a.ndim)
    sorted_rows = pl.BlockSpec((SORT_ROWS, d), lambda b, j: (b * nt + j, 0))
    picks = pl.BlockSpec((1, 1, 16, TILE), lambda b, j: (b, j, 0, 0))
    in_specs = ([tok(d), sorted_rows, picks, modspec] if has_prev else _split_specs(d)) + [
        modspec, full(nw), full(win), full(lb), full(tri), full(lvlm),
        pl.BlockSpec((1,) + pa.shape[1:], lambda b, j: (jnp.minimum(j, 1), 0, 0, 0)),
        pl.BlockSpec((1,) + pinv.shape[1:], lambda b, j: (jnp.minimum(j, 1), 0, 0)),
        full(wp), full(ps)]
    out_shape = [jax.ShapeDtypeStruct((bsz, ntok, HGRN_W), F32),
                 jax.ShapeDtypeStruct((bsz, ntok, 2 * HGRN_W), BF16),
                 jax.ShapeDtypeStruct((bsz, nt, 2, N_HEADS, HEAD_DIM, HEAD_DIM), F32),
                 jax.ShapeDtypeStruct((bsz, nt, 2, N_HEADS, 1, HEAD_DIM), F32),
                 jax.ShapeDtypeStruct((bsz, ntok, HGRN_W), BF16),
                 jax.ShapeDtypeStruct((bsz, ntok, POOL_W), BF16)]
    out_specs = [tok(HGRN_W), tok(2 * HGRN_W),
                 pl.BlockSpec((1, 1, 2, N_HEADS, HEAD_DIM, HEAD_DIM), lambda b, j: (b, j, 0, 0, 0, 0)),
                 pl.BlockSpec((1, 1, 2, N_HEADS, 1, HEAD_DIM), lambda b, j: (b, j, 0, 0, 0, 0)),
                 tok(HGRN_W), tok(POOL_W)]
    if has_prev:
        out_shape.append(jax.ShapeDtypeStruct((bsz, ntok, d), F32))
        out_specs.append(tok(d))
    args = list(xs) + [modx, nw, win, lb, tri, lvlm, pa, pinv, wp, ps]
    return pl.pallas_call(
        functools.partial(_mixer_in_kernel, has_prev),
        out_shape=out_shape, grid=(bsz, nt), in_specs=in_specs, out_specs=out_specs,
        scratch_shapes=[pltpu.VMEM((TILE, d), F32)] if has_prev else [],
        compiler_params=pltpu.CompilerParams(dimension_semantics=("parallel", "parallel"),
                                             vmem_limit_bytes=VMEM_LIMIT),
        name="mixer_in",
    )(*args)


def _scan_kernel(utf_ref, af_ref, utb_ref, ab_ref, sf_ref, sb_ref, accf, accb):
    @pl.when(pl.program_id(1) == 0)
    def _():
        accf[...] = jnp.zeros_like(accf)
        accb[...] = jnp.zeros_like(accb)

    sf_ref[0, 0] = accf[...].astype(BF16)
    sb_ref[0, 0] = accb[...].astype(BF16)
    accf[...] = af_ref[0, 0, 0] * accf[...] + utf_ref[0, 0, 0]
    accb[...] = ab_ref[0, 0, 0] * accb[...] + utb_ref[0, 0, 0]


def _scan_state(ut, a):
    bsz, nt = ut.shape[:2]
    bwd = lambda s: jnp.where(s == 0, 0, nt - s)
    st = (N_HEADS, HEAD_DIM, HEAD_DIM)
    ut_spec = lambda d, o: pl.BlockSpec((1, 1, 1) + st, lambda b, s: (b, o(s), d, 0, 0, 0))
    a_spec = lambda d, o: pl.BlockSpec((1, 1, 1, N_HEADS, 1, HEAD_DIM), lambda b, s: (b, o(s), d, 0, 0, 0))
    ident = lambda s: s
    return pl.pallas_call(
        _scan_kernel,
        out_shape=[jax.ShapeDtypeStruct((bsz, nt) + st, BF16)] * 2,
        grid=(bsz, nt),
        in_specs=[ut_spec(0, ident), a_spec(0, ident), ut_spec(1, bwd), a_spec(1, bwd)],
        out_specs=[pl.BlockSpec((1, 1) + st, lambda b, s: (b, s, 0, 0, 0)),
                   pl.BlockSpec((1, 1) + st, lambda b, s: (b, bwd(s), 0, 0, 0))],
        scratch_shapes=[pltpu.VMEM(st, F32), pltpu.VMEM(st, F32)],
        compiler_params=pltpu.CompilerParams(dimension_semantics=("parallel", "arbitrary"),
                                             vmem_limit_bytes=VMEM_LIMIT),
        name="scan_state",
    )(ut, a, ut, a)


def _route(scores, bias):
    e, t = scores.shape
    ng, per = N_EXPERT_GROUPS, EXPERTS_PER_GROUP
    biased = scores + bias
    grouped = biased.reshape(ng, per, t)
    member = lax.broadcasted_iota(jnp.int32, (ng, per, t), 1).astype(F32)
    m1 = jnp.max(grouped, axis=1, keepdims=True)
    i1 = jnp.min(jnp.where(grouped == m1, member, float(per)), axis=1, keepdims=True)
    m2 = jnp.max(jnp.where(member == i1, NEG_INF, grouped), axis=1, keepdims=True)
    work = jnp.broadcast_to(m1 + m2, (ng, per, t)).reshape(e, t)
    eidx = lax.broadcasted_iota(jnp.int32, (e, t), 0)
    gidx = (eidx >> 3).astype(F32)
    eidx = eidx.astype(F32)

    def first_max(x, idx):
        mx = jnp.max(x, axis=0, keepdims=True)
        return jnp.min(jnp.where(x == mx, idx, float(4 * e)), axis=0, keepdims=True)

    keep = jnp.zeros_like(biased)
    for _ in range(TOPK_GROUPS):
        hit = gidx == first_max(work, gidx)
        keep = jnp.where(hit, 1.0, keep)
        work = jnp.where(hit, NEG_INF, work)
    work = jnp.where(keep > 0.0, biased, MASK_SCORE)
    sel = jnp.zeros_like(biased)
    hits = []
    for _ in range(TOP_K):
        hit = eidx == first_max(work, eidx)
        hits.append(hit)
        sel = jnp.where(hit, 1.0, sel)
        work = jnp.where(hit, NEG_INF, work)
    gate = jnp.where(sel > 0.0, scores, 0.0)
    return gate / jnp.sum(gate, axis=0, keepdims=True) * ROUTED_SCALE, sel, hits


def _sort_rows(gates, sel, hits, h2b, tri_upper, xs_ref, posw_ref, cnt_ref):
    e, t = sel.shape
    selb = sel.astype(BF16)
    incl = _dot(selb, tri_upper)
    cnt = _dot(selb, jnp.ones((t, 128), BF16))
    ngran = jnp.floor((cnt + float(GRAN - 1)) * (1.0 / GRAN))
    ei = lax.broadcasted_iota(jnp.int32, (e, e), 0)
    ej = lax.broadcasted_iota(jnp.int32, (e, e), 1)
    earlier = jnp.where(ej < ei, 1.0, 0.0).astype(BF16)
    run_start = _dot(earlier, ngran.astype(BF16)) * float(GRAN)
    slot = run_start[:, 0:1] + incl - sel
    cnt_ref[0, 0] = cnt

    ridx = lax.broadcasted_iota(jnp.int32, (16, t), 0)
    posw = jnp.where(ridx < 8, -1.0, 0.0)
    for k, hit in enumerate(hits):
        pos_k = jnp.sum(jnp.where(hit, slot, 0.0), axis=0, keepdims=True)
        w_k = jnp.sum(jnp.where(hit, gates, 0.0), axis=0, keepdims=True)
        posw = jnp.where(ridx == k, pos_k, jnp.where(ridx == 8 + k, w_k, posw))
    posw_ref[0, 0] = posw

    row = lax.broadcasted_iota(jnp.int32, (TILE, t), 0).astype(F32).astype(BF16)
    one, zero = jnp.ones((TILE, t), BF16), jnp.zeros((TILE, t), BF16)
    used_rows = jnp.sum(ngran[:, 0:1]) * float(GRAN)
    def sort_block(rb):
        onehot = zero
        for rel in _block_positions(posw, rb):
            onehot = onehot + jnp.where(row == rel, one, zero)
        xs_ref[rb * TILE:(rb + 1) * TILE, :] = _dot(onehot, h2b).astype(BF16)

    for rb in range(ALWAYS_USED_BLOCKS):
        sort_block(rb)
    for rb in range(ALWAYS_USED_BLOCKS, SORT_ROWS // TILE):
        @pl.when(used_rows > float(rb * TILE))
        def _():
            sort_block(rb)

        @pl.when(used_rows <= float(rb * TILE))
        def _():
            xs_ref[rb * TILE:(rb + 1) * TILE, :] = jnp.zeros((TILE, h2b.shape[1]), BF16)


def _mixer_out_kernel(split, *refs):
    x_in = _tile_of(refs[0], refs[1]) if split else refs[0][0]
    (oi_ref, qe_ref, sf_ref, sb_ref, og_ref, po_ref, mod_ref, gw_ref, wo_ref, nw2_ref, wr_ref, rb_ref,
     sw1_ref, sw3_ref, sw2_ref, tri_ref, xb_ref, xs_ref, posw_ref, cnt_ref) = refs[2 if split else 1:]
    m = mod_ref[0, 0]
    parts = []
    for h in range(N_HEADS):
        sl = slice(h * HEAD_DIM, (h + 1) * HEAD_DIM)
        o = (oi_ref[0, :, sl]
             + _dot_nt(qe_ref[0, :, sl], sf_ref[0, 0, h])
             + _dot_nt(qe_ref[0, :, HGRN_W + h * HEAD_DIM:HGRN_W + (h + 1) * HEAD_DIM],
                       sb_ref[0, 0, h]))
        y = _rmsnorm(o, gw_ref[...]) * og_ref[0, :, sl].astype(F32)
        parts.append(y.astype(BF16))
    mix = jnp.concatenate(parts + [po_ref[0]], axis=-1)
    x1 = x_in + m[2:3] * _dot(mix, wo_ref[...])
    h2 = _rmsnorm(x1, nw2_ref[...]) * (1.0 + m[4:5]) + m[3:4]
    h2b = h2.astype(BF16)
    shared = _dot((_silu(_dot(h2b, sw1_ref[...])) * _dot(h2b, sw3_ref[...])).astype(BF16), sw2_ref[...])
    xb_ref[0] = x1 + m[5:6] * shared
    h2lo = (h2 - h2b.astype(F32)).astype(BF16)
    logits = _dot_nt(wr_ref[0], h2b) + _dot_nt(wr_ref[0], h2lo) + _dot_nt(wr_ref[1], h2b)
    gates, sel, hits = _route(_sigmoid(logits), rb_ref[...])
    _sort_rows(gates, sel, hits, h2b, tri_ref[1], xs_ref, posw_ref, cnt_ref)


def _mixer_out(x_parts, oi, qe, sf, sb, og, po, modx, gw, wo, nw2, wr, rb, sw1, sw3, sw2, tri):
    split = len(x_parts) == 2
    bsz, ntok, _ = oi.shape
    d = x_parts[0].shape[-1]
    nt = ntok // TILE
    tok = lambda w: pl.BlockSpec((1, TILE, w), lambda b, j: (b, j, 0))
    st = pl.BlockSpec((1, 1, N_HEADS, HEAD_DIM, HEAD_DIM), lambda b, j: (b, j, 0, 0, 0))
    full = lambda a: pl.BlockSpec(a.shape, lambda b, j: (0,) * a.ndim)
    return pl.pallas_call(
        functools.partial(_mixer_out_kernel, split),
        out_shape=[jax.ShapeDtypeStruct((bsz, ntok, d), F32),
                   jax.ShapeDtypeStruct((bsz * nt * SORT_ROWS, d), BF16),
                   jax.ShapeDtypeStruct((bsz, nt, 16, TILE), F32),
                   jax.ShapeDtypeStruct((bsz, nt, N_EXPERTS, 128), F32)],
        grid=(bsz, nt),
        in_specs=(_split_specs(d) if split else [tok(d)]) + [
                  tok(HGRN_W), tok(2 * HGRN_W), st, st, tok(HGRN_W), tok(POOL_W),
                  pl.BlockSpec((1, 1, 6, d), lambda b, j: (b, jnp.minimum(j, 1), 0, 0)),
                  full(gw), full(wo), full(nw2), full(wr), full(rb), full(sw1), full(sw3), full(sw2),
                  full(tri)],
        out_specs=[tok(d), pl.BlockSpec((SORT_ROWS, d), lambda b, j: (b * nt + j, 0)),
                   pl.BlockSpec((1, 1, 16, TILE), lambda b, j: (b, j, 0, 0)),
                   pl.BlockSpec((1, 1, N_EXPERTS, 128), lambda b, j: (b, j, 0, 0))],
        compiler_params=pltpu.CompilerParams(dimension_semantics=("parallel", "parallel"),
                                             vmem_limit_bytes=VMEM_LIMIT),
        name="mixer_out",
    )(*x_parts, oi, qe, sf, sb, og, po, modx, gw, wo, nw2, wr, rb, sw1, sw3, sw2, tri)


def _moe_plan(cnt, nb_max):
    ntile, ne = cnt.shape
    ngran = (cnt + GRAN - 1) // GRAN
    run_start = jnp.cumsum(ngran, axis=1) - ngran
    ngran_t = ngran.T
    per_e = ngran_t.sum(axis=1)
    nblk = (per_e + GRAN_PER_BLOCK - 1) // GRAN_PER_BLOCK
    blk_end = jnp.cumsum(nblk)
    blk_start = blk_end - nblk
    nb = blk_end[-1]
    bidx = jnp.arange(nb_max, dtype=jnp.int32)
    be = jnp.sum((blk_end[None, :] <= jnp.minimum(bidx, nb - 1)[:, None]).astype(jnp.int32), axis=1)
    q0 = (bidx - blk_start[be]) * GRAN_PER_BLOCK
    nvalid = jnp.where(bidx < nb, jnp.clip(per_e[be] - q0, 0, GRAN_PER_BLOCK), 0)
    run_hi = jnp.cumsum(ngran_t, axis=1)
    run_lo = run_hi - ngran_t
    base = jnp.arange(ntile, dtype=jnp.int32)[None, :] * GRAN_PER_TILE + run_start.T - run_lo
    q = q0[:, None] + jnp.arange(GRAN_PER_BLOCK, dtype=jnp.int32)[None, :]
    tile = jnp.sum((run_hi[be][:, None, :] <= q[:, :, None]).astype(jnp.int32), axis=-1)
    src = jnp.take_along_axis(base[be], jnp.minimum(tile, ntile - 1), axis=1) + q
    src = jnp.where(tile < ntile, src, src[:, 0:1])
    return (nvalid.astype(jnp.int32), be, nb.reshape(1).astype(jnp.int32),
            src.reshape(-1).astype(jnp.int32))


def _moe_ffn_kernel(nv_ref, be_ref, nb_ref, src_ref, xs_hbm, w1_ref, w3_ref, w2_ref, ys_hbm,
                    xin0, xin1, yout0, yout1, w1b, w3b, w2b, sem_in, sem_out):
    b = pl.program_id(0)
    nb = nb_ref[0]
    xin, yout = (xin0, xin1), (yout0, yout1)
    block_rows = GRAN_PER_BLOCK * GRAN
    full = lambda blk: nv_ref[blk] == GRAN_PER_BLOCK

    @pl.when((b == 0) | (be_ref[b] != be_ref[jnp.maximum(b - 1, 0)]))
    def _():
        w1b[...] = w1_ref[0, 0].astype(BF16)
        w3b[...] = w3_ref[0, 0].astype(BF16)
        w2b[...] = w2_ref[0, 0].astype(BF16)

    def start_gather(blk, sl):
        for i in range(GRAN_PER_BLOCK):
            pltpu.make_async_copy(xs_hbm.at[src_ref[blk * GRAN_PER_BLOCK + i]],
                                  xin[sl].at[pl.ds(i * GRAN, GRAN), :], sem_in.at[sl]).start()

    def wait_gather(sl):
        for i in range(GRAN_PER_BLOCK):
            pltpu.make_async_copy(xs_hbm.at[0], xin[sl].at[pl.ds(i * GRAN, GRAN), :], sem_in.at[sl]).wait()

    def granule_out(blk, sl, i):
        return pltpu.make_async_copy(yout[sl].at[pl.ds(i * GRAN, GRAN), :],
                                     ys_hbm.at[src_ref[blk * GRAN_PER_BLOCK + i]], sem_out.at[sl])

    def ffn(sl, lo, hi):
        x = xin[sl][lo:hi, :]
        hid = (_silu(_dot(x, w1b[...])) * _dot(x, w3b[...])).astype(BF16)
        yout[sl][lo:hi, :] = _dot(hid, w2b[...]).astype(BF16)

    def scatter(blk, sl, wait):
        nv = nv_ref[blk]
        for i in range(GRAN_PER_BLOCK):
            @pl.when(i < nv)
            def _():
                if wait:
                    granule_out(blk, sl, i).wait()
                else:
                    granule_out(blk, sl, i).start()

    def steady_step(sl):
        wait_gather(sl)

        @pl.when(full(b - 2))
        def _():
            for i in range(GRAN_PER_BLOCK):
                granule_out(b - 2, sl, i).wait()

        @pl.when(jnp.logical_not(full(b - 2)))
        def _():
            scatter(b - 2, sl, True)

        start_gather(b + 1, 1 - sl)
        half = block_rows // 2
        for part in range(2):
            ffn(sl, part * half, (part + 1) * half)
            for i in range(part * GRAN_PER_BLOCK // 2, (part + 1) * GRAN_PER_BLOCK // 2):
                granule_out(b, sl, i).start()

    def general_step(sl):
        @pl.when(b == 0)
        def _():
            start_gather(b, sl)

        @pl.when(b + 1 < nb)
        def _():
            start_gather(b + 1, 1 - sl)

        wait_gather(sl)

        @pl.when(b >= 2)
        def _():
            scatter(b - 2, sl, True)

        ffn(sl, 0, block_rows)
        scatter(b, sl, False)

        @pl.when(b == nb - 1)
        def _():
            scatter(b, sl, True)

            @pl.when(b >= 1)
            def _():
                scatter(b - 1, 1 - sl, True)

    steady = (b >= 2) & (b + 1 < nb) & full(b)
    for sl in range(2):
        @pl.when((b < nb) & (lax.rem(b, 2) == sl) & steady)
        def _():
            steady_step(sl)

        @pl.when((b < nb) & (lax.rem(b, 2) == sl) & jnp.logical_not(steady))
        def _():
            general_step(sl)


def _moe_ffn(xs, plan, w1, w3, w2, layer, nb_max):
    nrows, d = xs.shape
    f = w1.shape[-1]
    rows = GRAN_PER_BLOCK * GRAN
    wspec = lambda shape: pl.BlockSpec((1, 1) + shape, lambda b, nv, be, nb, src: (layer, be[b], 0, 0))
    ys = pl.pallas_call(
        _moe_ffn_kernel,
        out_shape=jax.ShapeDtypeStruct((nrows // GRAN, GRAN, d), BF16),
        grid_spec=pltpu.PrefetchScalarGridSpec(
            num_scalar_prefetch=4, grid=(nb_max,),
            in_specs=[pl.BlockSpec(memory_space=pl.ANY), wspec((d, f)), wspec((d, f)), wspec((f, d))],
            out_specs=pl.BlockSpec(memory_space=pl.ANY),
            scratch_shapes=[pltpu.VMEM((rows, d), BF16)] * 4
            + [pltpu.VMEM((d, f), BF16), pltpu.VMEM((d, f), BF16), pltpu.VMEM((f, d), BF16),
               pltpu.SemaphoreType.DMA((2,)), pltpu.SemaphoreType.DMA((2,))]),
        input_output_aliases={4: 0},
        compiler_params=pltpu.CompilerParams(dimension_semantics=("arbitrary",),
                                             vmem_limit_bytes=VMEM_LIMIT),
        name="moe_ffn",
    )(*plan, xs.reshape(nrows // GRAN, GRAN, d), w1, w3, w2)
    return ys.reshape(nrows, d)


def _final_kernel(xb_ref, ys_ref, posw_ref, mod_ref, nw_ref, o_ref, acc_ref):
    x = xb_ref[0] + mod_ref[0, 0][5:6] * _combine(ys_ref, posw_ref[0, 0], acc_ref)
    o_ref[0] = _rmsnorm(x, nw_ref[...])


def _final(xb, ys, posw, modx, nw, n_ctx_tiles):
    bsz, ntok, d = xb.shape
    nt_all = ntok // TILE
    nt = nt_all - n_ctx_tiles
    lat = lambda w: pl.BlockSpec((1, TILE, w), lambda b, j: (b, j + n_ctx_tiles, 0))
    return pl.pallas_call(
        _final_kernel,
        out_shape=jax.ShapeDtypeStruct((bsz, nt * TILE, d), F32),
        grid=(bsz, nt),
        scratch_shapes=[pltpu.VMEM((TILE, d), F32)],
        in_specs=[lat(d), pl.BlockSpec((SORT_ROWS, d), lambda b, j: (b * nt_all + j + n_ctx_tiles, 0)),
                  pl.BlockSpec((1, 1, 16, TILE), lambda b, j: (b, j + n_ctx_tiles, 0, 0)),
                  pl.BlockSpec((1, 1, 6, d), lambda b, j: (b, 1, 0, 0)),
                  pl.BlockSpec(nw.shape, lambda b, j: (0, 0))],
        out_specs=pl.BlockSpec((1, TILE, d), lambda b, j: (b, j, 0)),
        compiler_params=pltpu.CompilerParams(dimension_semantics=("parallel", "parallel"),
                                             vmem_limit_bytes=VMEM_LIMIT),
        name="final_norm",
    )(xb, ys, posw, modx, nw)


def kernel(x, c, ctx, c_ctx, norm1_w, norm2_w, w_ada, b_ada, w_in, hgrn_lb_logits, hgrn_gnorm_w, w_pool,
           pool_scale, w_out, w_router, router_bias, w1_experts, w3_experts, w2_experts, w1_shared,
           w3_shared, w2_shared, final_norm_w):
    bsz, seq_len, d = x.shape
    ctx_len = ctx.shape[1]
    depth = w_ada.shape[0]
    assert ctx_len == TILE and seq_len % TILE == 0 and seq_len % GRID_W == 0
    ntok = ctx_len + seq_len
    ntile = bsz * ntok // TILE
    nb_max = -(-ntile * MAX_GRAN_PER_TILE // GRAN_PER_BLOCK) + N_EXPERTS

    lf, lbm = _level_matrices()
    lower, upper = _tri_matrices()
    pa, pinv = _pool_matrices()
    consts = (jnp.asarray(np.stack([lower, upper]), BF16), jnp.asarray(np.maximum(lf, lbm)),
              jnp.asarray(pa, BF16), jnp.asarray(pinv))

    p_lb = jax.nn.softmax(hgrn_lb_logits.astype(F32), axis=0)
    lb_all = jnp.cumsum(p_lb, axis=0) - p_lb[0:1]

    rows = 16
    cvec = jnp.concatenate([c, c_ctx[None, :], jnp.zeros((rows - bsz - 1, d), F32)], axis=0)
    mod = _ada_table(cvec, w_ada, b_ada)
    mod_lat = mod[:, :bsz].reshape(depth, bsz, 6, d)
    mod_ctx = jnp.broadcast_to(mod[:, bsz].reshape(depth, 1, 6, d), (depth, bsz, 6, d))
    modx = jnp.stack([mod_ctx, mod_lat], axis=2)

    xs = (ctx, x)
    for l in range(depth):
        has_prev = l > 0
        outs = _mixer_in(has_prev, xs + ((modx[l - 1],) if has_prev else ()), modx[l], norm1_w[l][None, :],
                         w_in[l].astype(BF16), lb_all[l], consts, w_pool[l].astype(BF16),
                         pool_scale[l][None, :])
        oi, qe, ut, a, og, po = outs[:6]
        x_cur = (outs[6],) if has_prev else xs
        sf, sb = _scan_state(ut, a)
        wr_t = w_router[l].T
        wr_hi = wr_t.astype(BF16)
        xb, rows_sorted, posw, cnt = _mixer_out(
            x_cur, oi, qe, sf, sb, og, po, modx[l], hgrn_gnorm_w[l][None, :], w_out[l].astype(BF16),
            norm2_w[l][None, :], jnp.stack([wr_hi, (wr_t - wr_hi.astype(F32)).astype(BF16)]),
            router_bias[l][:, None], w1_shared[l].astype(BF16),
            w3_shared[l].astype(BF16), w2_shared[l].astype(BF16), consts[0])
        plan = _moe_plan(cnt[:, :, :, 0].reshape(ntile, N_EXPERTS).astype(jnp.int32), nb_max)
        ys = _moe_ffn(rows_sorted, plan, w1_experts, w3_experts, w2_experts, l, nb_max)
        xs = (xb, ys, posw)
    return _final(xs[0], xs[1], xs[2], modx[depth - 1], final_norm_w[None, :], ctx_len // TILE)
```

```python
import functools

import numpy as np
import jax
import jax.numpy as jnp
from jax import lax
from jax.experimental import pallas as pl
from jax.experimental.pallas import tpu as pltpu

F32 = jnp.float32
BF16 = jnp.bfloat16

EPS = 1e-6
LB_FLOOR = 1e-30
MASK_SCORE = -1e4
GRID_W = 64
HEAD_DIM = 128
N_HEADS = 4
HGRN_W = N_HEADS * HEAD_DIM
POOL_WINDOWS = (2, 4, 8, 16)
POOL_GROUPS = len(POOL_WINDOWS)
POOL_GROUP_DIM = 128
POOL_W = POOL_GROUPS * POOL_GROUP_DIM
N_EXPERTS = 64
TOP_K = 6
N_EXPERT_GROUPS = 8
EXPERTS_PER_GROUP = N_EXPERTS // N_EXPERT_GROUPS
TOPK_GROUPS = 4
ROUTED_SCALE = 2.5

TILE = 256
N_LEVELS = 8
GRAN = 16
MAX_GRAN_PER_TILE = (TILE * TOP_K + N_EXPERTS * (GRAN - 1)) // GRAN
GRAN_PER_TILE = -(-MAX_GRAN_PER_TILE * GRAN // TILE) * TILE // GRAN
SORT_ROWS = GRAN_PER_TILE * GRAN
ALWAYS_USED_BLOCKS = TILE * TOP_K // TILE + 2
GRAN_PER_BLOCK = 64
VMEM_LIMIT = 56 * 1024 * 1024
NEG_INF = float("-inf")
LOG2E = 1.4426950408889634


def _sigmoid(z):
    return 1.0 / (1.0 + jnp.exp(-z))


def _silu(z):
    return z * _sigmoid(z)


def _dot(a, b):
    return jnp.dot(a, b, preferred_element_type=F32)


def _dot_nt(a, b):
    return lax.dot_general(a, b, (((1,), (1,)), ((), ())), preferred_element_type=F32)


def _dot_tn(a, b):
    return lax.dot_general(a, b, (((0,), (0,)), ((), ())), preferred_element_type=F32)


def _split3(g):
    hi = g.astype(BF16)
    r1 = g - hi.astype(F32)
    mid = r1.astype(BF16)
    lo = (r1 - mid.astype(F32)).astype(BF16)
    return hi, mid, lo


def _rmsnorm(x, w):
    return x * lax.rsqrt(jnp.mean(x * x, axis=-1, keepdims=True) + EPS) * w


def _level_matrices():
    t = np.arange(TILE)
    xor = t[:, None] ^ t[None, :]
    lvl = np.floor(np.log2(np.maximum(xor, 1))).astype(np.int32)
    lf = np.where(t[:, None] > t[None, :], lvl, -1).astype(np.int32)
    return lf, lf.T.copy()


def _tri_matrices():
    t = np.arange(TILE)
    lower = (t[:, None] >= t[None, :]).astype(np.float32)
    return lower, lower.T.copy()


def _pool_matrices():
    a = np.zeros((2, POOL_GROUPS, TILE, TILE), np.float32)
    inv = np.zeros((2, TILE, POOL_W), np.float32)
    t = np.arange(TILE)
    for kind, n in ((0, TILE), (1, GRID_W)):
        pos, base = t % n, (t // n) * n
        for gi, w in enumerate(POOL_WINDOWS):
            lo = np.clip(pos - w // 2, 0, n)
            hi = np.clip(pos + w - w // 2, 0, n)
            s = t[None, :]
            a[kind, gi] = ((s >= (base + lo)[:, None]) & (s < (base + hi)[:, None])).astype(np.float32)
            inv[kind, :, gi * POOL_GROUP_DIM:(gi + 1) * POOL_GROUP_DIM] = (1.0 / (hi - lo))[:, None]
    return a, inv


def _ada_kernel(c_ref, w_ref, b_ref, o_ref):
    o_ref[0] = jnp.dot(_silu(c_ref[...]), w_ref[0], preferred_element_type=F32,
                       precision=lax.Precision.HIGHEST) + b_ref[0]


def _ada_table(cvec, w_ada, b_ada):
    depth, d, n = w_ada.shape
    nb = 4
    rows = cvec.shape[0]
    return pl.pallas_call(
        _ada_kernel,
        out_shape=jax.ShapeDtypeStruct((depth, rows, n), F32),
        grid=(depth, nb),
        in_specs=[pl.BlockSpec((rows, d), lambda l, j: (0, 0)),
                  pl.BlockSpec((1, d, n // nb), lambda l, j: (l, 0, j)),
                  pl.BlockSpec((1, 1, n // nb), lambda l, j: (l, 0, j))],
        out_specs=pl.BlockSpec((1, rows, n // nb), lambda l, j: (l, 0, j)),
        compiler_params=pltpu.CompilerParams(dimension_semantics=("arbitrary", "arbitrary"),
                                             vmem_limit_bytes=VMEM_LIMIT),
        name="ada_table",
    )(cvec, w_ada, b_ada.reshape(depth, 1, n))


def _reference_rows(b, lvl, backward):
    w = 1 << lvl
    pm = w if backward else w - 1
    if w >= 4:
        n = TILE // (2 * w)
        row = b.reshape(n, 2 * w, HEAD_DIM)[:, pm:pm + 1, :]
        return jnp.broadcast_to(row, (n, 2 * w, HEAD_DIM)).reshape(TILE, HEAD_DIM)
    pos = lax.broadcasted_iota(jnp.int32, (TILE, HEAD_DIM), 0) & (2 * w - 1)
    out = b
    for p in range(2 * w):
        if p == pm:
            continue
        out = jnp.where(pos == p, pltpu.roll(b, (p - pm) % TILE, axis=0), out)
    return out


def _block_positions(posw, rb):
    rel = posw[0:8, :] - float(rb * TILE)
    rel = jnp.where((rel >= 0.0) & (rel < float(TILE)), rel, -1.0).astype(BF16)
    return [rel[k:k + 1, :] for k in range(TOP_K)]


def _combine(ys_ref, posw, acc_ref):
    row = lax.broadcasted_iota(jnp.int32, (TILE, TILE), 0).astype(F32).astype(BF16)
    wts = posw[8:16, :].astype(BF16)
    zero = jnp.zeros((TILE, TILE), BF16)
    last_pos = jnp.max(posw[0:8, :])

    def block(rb):
        pw = zero
        for k, rel in enumerate(_block_positions(posw, rb)):
            pw = pw + jnp.where(row == rel, wts[k:k + 1, :], zero)
        return _dot_tn(pw, ys_ref[rb * TILE:(rb + 1) * TILE, :])

    acc = block(0)
    for rb in range(1, ALWAYS_USED_BLOCKS):
        acc = acc + block(rb)
    acc_ref[...] = acc
    for rb in range(ALWAYS_USED_BLOCKS, SORT_ROWS // TILE):
        @pl.when(last_pos >= float(rb * TILE))
        def _():
            acc_ref[...] += block(rb)
    return acc_ref[...]


def _mixer_in_kernel(has_prev, *refs):
    if has_prev:
        xb_ref, ys_ref, posw_ref, modp_ref = refs[:4]
        refs = refs[4:]
    else:
        ctx_ref, x_ref = refs[:2]
        refs = refs[2:]
    (mod_ref, nw_ref, win_ref, lb_ref, tri_ref, lvl_ref, pa_ref, pinv_ref, wp_ref, ps_ref) = refs[:10]
    outs = refs[10:]
    if has_prev:
        oi_ref, qe_ref, ut_ref, a_ref, og_ref, po_ref, xo_ref, acc_ref = outs
        x = xb_ref[0] + modp_ref[0, 0][5:6] * _combine(ys_ref, posw_ref[0, 0], acc_ref)
        xo_ref[0] = x
    else:
        oi_ref, qe_ref, ut_ref, a_ref, og_ref, po_ref = outs
        x = _tile_of(ctx_ref, x_ref)

    m = mod_ref[0, 0]
    hx = (_rmsnorm(x, nw_ref[...]) * (1.0 + m[1:2]) + m[0:1]).astype(BF16)

    def proj(col):
        return _dot(hx, win_ref[:, col * HGRN_W:(col + 1) * HGRN_W])

    v = proj(2).astype(BF16)
    q = _silu(proj(3)) * (HEAD_DIM ** -0.5)
    og_ref[0] = _silu(proj(4)).astype(BF16)

    p = proj(5)
    pb = p.astype(BF16)
    pinv = pinv_ref[0]
    for g in range(POOL_GROUPS):
        sl = slice(g * POOL_GROUP_DIM, (g + 1) * POOL_GROUP_DIM)
        pooled = _dot(pa_ref[0, g], pb[:, sl]) * pinv[:, sl] - p[:, sl]
        po_ref[0, :, sl] = (_dot(pooled.astype(BF16), wp_ref[g]) * ps_ref[:, sl]).astype(BF16)

    ks, bs = [], []
    for d in range(2):
        z = proj(d)
        lb = lb_ref[d:d + 1, :]
        sg = _sigmoid(z)
        kk = (1.0 - lb) * (1.0 - sg)
        g = jnp.log(jnp.maximum(lb, LB_FLOOR) + (1.0 - lb) * sg)
        hi, mid, lo = _split3(g)
        tri = tri_ref[d]
        b = _dot(tri, hi) + _dot(tri, mid) + _dot(tri, lo)
        b_tot = b[TILE - 1:TILE, :] if d == 0 else b[0:1, :]
        qe_ref[0, :, d * HGRN_W:(d + 1) * HGRN_W] = (q * jnp.exp(b)).astype(BF16)
        k_end = (kk * jnp.exp(b_tot - b)).astype(BF16)
        a_tot = jnp.exp(b_tot)
        for h in range(N_HEADS):
            sl = slice(h * HEAD_DIM, (h + 1) * HEAD_DIM)
            ut_ref[0, 0, d, h] = _dot_tn(v[:, sl], k_end[:, sl])
            a_ref[0, 0, d, h] = a_tot[:, sl]
        ks.append(kk)
        bs.append(b)

    rows = lax.broadcasted_iota(jnp.int32, (TILE, HEAD_DIM), 0)
    upper_b16 = [(((rows >> lvl) & 1).astype(F32)).astype(BF16) > 0 for lvl in range(N_LEVELS)]
    eye = (lax.broadcasted_iota(jnp.int32, (TILE, TILE), 0)
           == lax.broadcasted_iota(jnp.int32, (TILE, TILE), 1))
    lmat = lvl_ref[...]
    zero = jnp.zeros((TILE, HEAD_DIM), BF16)
    for h in range(N_HEADS):
        sl = slice(h * HEAD_DIM, (h + 1) * HEAD_DIM)
        qh = q[:, sl]
        qb, kf, kb = qh.astype(BF16), ks[0][:, sl].astype(BF16), ks[1][:, sl].astype(BF16)
        bf, bb = bs[0][:, sl] * LOG2E, bs[1][:, sl] * LOG2E
        diag = jnp.sum(qh * (ks[0][:, sl] + ks[1][:, sl]), axis=-1, keepdims=True)
        scores = jnp.where(eye, diag, 0.0)
        for lvl in range(N_LEVELS):
            upb = upper_b16[lvl]
            arg_f = -jnp.abs(bf - _reference_rows(bf, lvl, False))
            arg_b = -jnp.abs(bb - _reference_rows(bb, lvl, True))
            xf = jnp.where(upb, qb, kf) * jnp.exp2(arg_f.astype(BF16))
            xb = jnp.where(upb, kb, qb) * jnp.exp2(arg_b.astype(BF16))
            lhs = jnp.concatenate([jnp.where(upb, xf, zero), jnp.where(upb, zero, xb)], axis=-1)
            rhs = jnp.concatenate([xf, xb], axis=-1)
            scores = jnp.where(lmat == lvl, _dot_nt(lhs, rhs), scores)
        oi_ref[0, :, sl] = _dot(scores.astype(BF16), v[:, sl])


def _tile_of(ctx_ref, x_ref):
    return jnp.where(pl.program_id(1) == 0, ctx_ref[0], x_ref[0])


def _split_specs(d):
    return [pl.BlockSpec((1, TILE, d), lambda b, j: (b, 0, 0)),
            pl.BlockSpec((1, TILE, d), lambda b, j: (b, jnp.maximum(j - 1, 0), 0))]


def _mixer_in(has_prev, xs, modx, nw, win, lb, consts, wp, ps):
    tri, lvlm, pa, pinv = consts
    bsz, _, d = xs[0].shape
    ntok = xs[0].shape[1] if has_prev else xs[0].shape[1] + xs[1].shape[1]
    nt = ntok // TILE
    tok = lambda w: pl.BlockSpec((1, TILE, w), lambda b, j: (b, j, 0))
    modspec = pl.BlockSpec((1, 1, 6, d), lambda b, j: (b, jnp.minimum(j, 1), 0, 0))
    full = lambda a: pl.BlockSpec(a.shape, lambda b, j: (0,) * a.ndim)
    sorted_rows = pl.BlockSpec((SORT_ROWS, d), lambda b, j: (b * nt + j, 0))
    picks = pl.BlockSpec((1, 1, 16, TILE), lambda b, j: (b, j, 0, 0))
    in_specs = ([tok(d), sorted_rows, picks, modspec] if has_prev else _split_specs(d)) + [
        modspec, full(nw), full(win), full(lb), full(tri), full(lvlm),
        pl.BlockSpec((1,) + pa.shape[1:], lambda b, j: (jnp.minimum(j, 1), 0, 0, 0)),
        pl.BlockSpec((1,) + pinv.shape[1:], lambda b, j: (jnp.minimum(j, 1), 0, 0)),
        full(wp), full(ps)]
    out_shape = [jax.ShapeDtypeStruct((bsz, ntok, HGRN_W), F32),
                 jax.ShapeDtypeStruct((bsz, ntok, 2 * HGRN_W), BF16),
                 jax.ShapeDtypeStruct((bsz, nt, 2, N_HEADS, HEAD_DIM, HEAD_DIM), F32),
                 jax.ShapeDtypeStruct((bsz, nt, 2, N_HEADS, 1, HEAD_DIM), F32),
                 jax.ShapeDtypeStruct((bsz, ntok, HGRN_W), BF16),
                 jax.ShapeDtypeStruct((bsz, ntok, POOL_W), BF16)]
    out_specs = [tok(HGRN_W), tok(2 * HGRN_W),
                 pl.BlockSpec((1, 1, 2, N_HEADS, HEAD_DIM, HEAD_DIM), lambda b, j: (b, j, 0, 0, 0, 0)),
                 pl.BlockSpec((1, 1, 2, N_HEADS, 1, HEAD_DIM), lambda b, j: (b, j, 0, 0, 0, 0)),
                 tok(HGRN_W), tok(POOL_W)]
    if has_prev:
        out_shape.append(jax.ShapeDtypeStruct((bsz, ntok, d), F32))
        out_specs.append(tok(d))
    args = list(xs) + [modx, nw, win, lb, tri, lvlm, pa, pinv, wp, ps]
    return pl.pallas_call(
        functools.partial(_mixer_in_kernel, has_prev),
        out_shape=out_shape, grid=(bsz, nt), in_specs=in_specs, out_specs=out_specs,
        scratch_shapes=[pltpu.VMEM((TILE, d), F32)] if has_prev else [],
        compiler_params=pltpu.CompilerParams(dimension_semantics=("parallel", "parallel"),
                                             vmem_limit_bytes=VMEM_LIMIT),
        name="mixer_in",
    )(*args)


def _scan_kernel(utf_ref, af_ref, utb_ref, ab_ref, sf_ref, sb_ref, accf, accb):
    @pl.when(pl.program_id(1) == 0)
    def _():
        accf[...] = jnp.zeros_like(accf)
        accb[...] = jnp.zeros_like(accb)

    sf_ref[0, 0] = accf[...].astype(BF16)
    sb_ref[0, 0] = accb[...].astype(BF16)
    accf[...] = af_ref[0, 0, 0] * accf[...] + utf_ref[0, 0, 0]
    accb[...] = ab_ref[0, 0, 0] * accb[...] + utb_ref[0, 0, 0]


def _scan_state(ut, a):
    bsz, nt = ut.shape[:2]
    bwd = lambda s: jnp.where(s == 0, 0, nt - s)
    st = (N_HEADS, HEAD_DIM, HEAD_DIM)
    ut_spec = lambda d, o: pl.BlockSpec((1, 1, 1) + st, lambda b, s: (b, o(s), d, 0, 0, 0))
    a_spec = lambda d, o: pl.BlockSpec((1, 1, 1, N_HEADS, 1, HEAD_DIM), lambda b, s: (b, o(s), d, 0, 0, 0))
    ident = lambda s: s
    return pl.pallas_call(
        _scan_kernel,
        out_shape=[jax.ShapeDtypeStruct((bsz, nt) + st, BF16)] * 2,
        grid=(bsz, nt),
        in_specs=[ut_spec(0, ident), a_spec(0, ident), ut_spec(1, bwd), a_spec(1, bwd)],
        out_specs=[pl.BlockSpec((1, 1) + st, lambda b, s: (b, s, 0, 0, 0)),
                   pl.BlockSpec((1, 1) + st, lambda b, s: (b, bwd(s), 0, 0, 0))],
        scratch_shapes=[pltpu.VMEM(st, F32), pltpu.VMEM(st, F32)],
        compiler_params=pltpu.CompilerParams(dimension_semantics=("parallel", "arbitrary"),
                                             vmem_limit_bytes=VMEM_LIMIT),
        name="scan_state",
    )(ut, a, ut, a)


def _route(scores, bias):
    e, t = scores.shape
    ng, per = N_EXPERT_GROUPS, EXPERTS_PER_GROUP
    biased = scores + bias
    grouped = biased.reshape(ng, per, t)
    member = lax.broadcasted_iota(jnp.int32, (ng, per, t), 1).astype(F32)
    m1 = jnp.max(grouped, axis=1, keepdims=True)
    i1 = jnp.min(jnp.where(grouped == m1, member, float(per)), axis=1, keepdims=True)
    m2 = jnp.max(jnp.where(member == i1, NEG_INF, grouped), axis=1, keepdims=True)
    work = jnp.broadcast_to(m1 + m2, (ng, per, t)).reshape(e, t)
    eidx = lax.broadcasted_iota(jnp.int32, (e, t), 0)
    gidx = (eidx >> 3).astype(F32)
    eidx = eidx.astype(F32)

    def first_max(x, idx):
        mx = jnp.max(x, axis=0, keepdims=True)
        return jnp.min(jnp.where(x == mx, idx, float(4 * e)), axis=0, keepdims=True)

    keep = jnp.zeros_like(biased)
    for _ in range(TOPK_GROUPS):
        hit = gidx == first_max(work, gidx)
        keep = jnp.where(hit, 1.0, keep)
        work = jnp.where(hit, NEG_INF, work)
    work = jnp.where(keep > 0.0, biased, MASK_SCORE)
    sel = jnp.zeros_like(biased)
    hits = []
    for _ in range(TOP_K):
        hit = eidx == first_max(work, eidx)
        hits.append(hit)
        sel = jnp.where(hit, 1.0, sel)
        work = jnp.where(hit, NEG_INF, work)
    gate = jnp.where(sel > 0.0, scores, 0.0)
    return gate / jnp.sum(gate, axis=0, keepdims=True) * ROUTED_SCALE, sel, hits


def _sort_rows(gates, sel, hits, h2b, tri_upper, xs_ref, posw_ref, cnt_ref):
    e, t = sel.shape
    selb = sel.astype(BF16)
    incl = _dot(selb, tri_upper)
    cnt = _dot(selb, jnp.ones((t, 128), BF16))
    ngran = jnp.floor((cnt + float(GRAN - 1)) * (1.0 / GRAN))
    ei = lax.broadcasted_iota(jnp.int32, (e, e), 0)
    ej = lax.broadcasted_iota(jnp.int32, (e, e), 1)
    earlier = jnp.where(ej < ei, 1.0, 0.0).astype(BF16)
    run_start = _dot(earlier, ngran.astype(BF16)) * float(GRAN)
    slot = run_start[:, 0:1] + incl - sel
    cnt_ref[0, 0] = cnt

    ridx = lax.broadcasted_iota(jnp.int32, (16, t), 0)
    posw = jnp.where(ridx < 8, -1.0, 0.0)
    for k, hit in enumerate(hits):
        pos_k = jnp.sum(jnp.where(hit, slot, 0.0), axis=0, keepdims=True)
        w_k = jnp.sum(jnp.where(hit, gates, 0.0), axis=0, keepdims=True)
        posw = jnp.where(ridx == k, pos_k, jnp.where(ridx == 8 + k, w_k, posw))
    posw_ref[0, 0] = posw

    row = lax.broadcasted_iota(jnp.int32, (TILE, t), 0).astype(F32).astype(BF16)
    one, zero = jnp.ones((TILE, t), BF16), jnp.zeros((TILE, t), BF16)
    used_rows = jnp.sum(ngran[:, 0:1]) * float(GRAN)
    def sort_block(rb):
        onehot = zero
        for rel in _block_positions(posw, rb):
            onehot = onehot + jnp.where(row == rel, one, zero)
        xs_ref[rb * TILE:(rb + 1) * TILE, :] = _dot(onehot, h2b).astype(BF16)

    for rb in range(ALWAYS_USED_BLOCKS):
        sort_block(rb)
    for rb in range(ALWAYS_USED_BLOCKS, SORT_ROWS // TILE):
        @pl.when(used_rows > float(rb * TILE))
        def _():
            sort_block(rb)

        @pl.when(used_rows <= float(rb * TILE))
        def _():
            xs_ref[rb * TILE:(rb + 1) * TILE, :] = jnp.zeros((TILE, h2b.shape[1]), BF16)


def _mixer_out_kernel(split, *refs):
    x_in = _tile_of(refs[0], refs[1]) if split else refs[0][0]
    (oi_ref, qe_ref, sf_ref, sb_ref, og_ref, po_ref, mod_ref, gw_ref, wo_ref, nw2_ref, wr_ref, rb_ref,
     sw1_ref, sw3_ref, sw2_ref, tri_ref, xb_ref, xs_ref, posw_ref, cnt_ref) = refs[2 if split else 1:]
    m = mod_ref[0, 0]
    parts = []
    for h in range(N_HEADS):
        sl = slice(h * HEAD_DIM, (h + 1) * HEAD_DIM)
        o = (oi_ref[0, :, sl]
             + _dot_nt(qe_ref[0, :, sl], sf_ref[0, 0, h])
             + _dot_nt(qe_ref[0, :, HGRN_W + h * HEAD_DIM:HGRN_W + (h + 1) * HEAD_DIM],
                       sb_ref[0, 0, h]))
        y = _rmsnorm(o, gw_ref[...]) * og_ref[0, :, sl].astype(F32)
        parts.append(y.astype(BF16))
    mix = jnp.concatenate(parts + [po_ref[0]], axis=-1)
    x1 = x_in + m[2:3] * _dot(mix, wo_ref[...])
    h2 = _rmsnorm(x1, nw2_ref[...]) * (1.0 + m[4:5]) + m[3:4]
    h2b = h2.astype(BF16)
    shared = _dot((_silu(_dot(h2b, sw1_ref[...])) * _dot(h2b, sw3_ref[...])).astype(BF16), sw2_ref[...])
    xb_ref[0] = x1 + m[5:6] * shared
    h2lo = (h2 - h2b.astype(F32)).astype(BF16)
    logits = _dot_nt(wr_ref[0], h2b) + _dot_nt(wr_ref[0], h2lo) + _dot_nt(wr_ref[1], h2b)
    gates, sel, hits = _route(_sigmoid(logits), rb_ref[...])
    _sort_rows(gates, sel, hits, h2b, tri_ref[1], xs_ref, posw_ref, cnt_ref)


def _mixer_out(x_parts, oi, qe, sf, sb, og, po, modx, gw, wo, nw2, wr, rb, sw1, sw3, sw2, tri):
    split = len(x_parts) == 2
    bsz, ntok, _ = oi.shape
    d = x_parts[0].shape[-1]
    nt = ntok // TILE
    tok = lambda w: pl.BlockSpec((1, TILE, w), lambda b, j: (b, j, 0))
    st = pl.BlockSpec((1, 1, N_HEADS, HEAD_DIM, HEAD_DIM), lambda b, j: (b, j, 0, 0, 0))
    full = lambda a: pl.BlockSpec(a.shape, lambda b, j: (0,) * a.ndim)
    return pl.pallas_call(
        functools.partial(_mixer_out_kernel, split),
        out_shape=[jax.ShapeDtypeStruct((bsz, ntok, d), F32),
                   jax.ShapeDtypeStruct((bsz * nt * SORT_ROWS, d), BF16),
                   jax.ShapeDtypeStruct((bsz, nt, 16, TILE), F32),
                   jax.ShapeDtypeStruct((bsz, nt, N_EXPERTS, 128), F32)],
        grid=(bsz, nt),
        in_specs=(_split_specs(d) if split else [tok(d)]) + [
                  tok(HGRN_W), tok(2 * HGRN_W), st, st, tok(HGRN_W), tok(POOL_W),
                  pl.BlockSpec((1, 1, 6, d), lambda b, j: (b, jnp.minimum(j, 1), 0, 0)),
                  full(gw), full(wo), full(nw2), full(wr), full(rb), full(sw1), full(sw3), full(sw2),
                  full(tri)],
        out_specs=[tok(d), pl.BlockSpec((SORT_ROWS, d), lambda b, j: (b * nt + j, 0)),
                   pl.BlockSpec((1, 1, 16, TILE), lambda b, j: (b, j, 0, 0)),
                   pl.BlockSpec((1, 1, N_EXPERTS, 128), lambda b, j: (b, j, 0, 0))],
        compiler_params=pltpu.CompilerParams(dimension_semantics=("parallel", "parallel"),
                                             vmem_limit_bytes=VMEM_LIMIT),
        name="mixer_out",
    )(*x_parts, oi, qe, sf, sb, og, po, modx, gw, wo, nw2, wr, rb, sw1, sw3, sw2, tri)


def _moe_plan(cnt, nb_max):
    ntile, ne = cnt.shape
    ngran = (cnt + GRAN - 1) // GRAN
    run_start = jnp.cumsum(ngran, axis=1) - ngran
    ngran_t = ngran.T
    per_e = ngran_t.sum(axis=1)
    nblk = (per_e + GRAN_PER_BLOCK - 1) // GRAN_PER_BLOCK
    blk_end = jnp.cumsum(nblk)
    blk_start = blk_end - nblk
    nb = blk_end[-1]
    per_e_bound = 4096
    assert ntile * (TILE // GRAN) < per_e_bound and (nb_max + 1) * per_e_bound * ne < 2 ** 31
    bidx = jnp.arange(nb_max, dtype=jnp.int32)
    key = jnp.where(nblk > 0, (blk_start * per_e_bound + per_e) * ne + jnp.arange(ne, dtype=jnp.int32), 0)
    placed = jnp.max(jnp.where(blk_start[None, :] == bidx[:, None], key[None, :], 0), axis=1)
    carried = jnp.max(jnp.where(bidx[None, :] <= bidx[:, None], placed[None, :], 0), axis=1)
    be = carried % ne
    q0 = (bidx - carried // (ne * per_e_bound)) * GRAN_PER_BLOCK
    nvalid = jnp.where(bidx < nb, jnp.clip((carried // ne) % per_e_bound - q0, 0, GRAN_PER_BLOCK), 0)
    run_hi = jnp.cumsum(ngran_t, axis=1)
    run_lo = run_hi - ngran_t
    base = jnp.arange(ntile, dtype=jnp.int32)[None, :] * GRAN_PER_TILE + run_start.T - run_lo
    q = q0[:, None] + jnp.arange(GRAN_PER_BLOCK, dtype=jnp.int32)[None, :]
    tile = jnp.sum((run_hi[be][:, None, :] <= q[:, :, None]).astype(jnp.int32), axis=-1)
    src = jnp.take_along_axis(base[be], jnp.minimum(tile, ntile - 1), axis=1) + q
    src = jnp.where(tile < ntile, src, src[:, 0:1])
    return (nvalid.astype(jnp.int32), be, nb.reshape(1).astype(jnp.int32),
            src.reshape(-1).astype(jnp.int32))


def _moe_ffn_kernel(nv_ref, be_ref, nb_ref, src_ref, xs_hbm, w1_ref, w3_ref, w2_ref, ys_hbm,
                    xin0, xin1, yout0, yout1, w1b, w3b, w2b, sem_in, sem_out):
    b = pl.program_id(0)
    nb = nb_ref[0]
    xin, yout = (xin0, xin1), (yout0, yout1)
    block_rows = GRAN_PER_BLOCK * GRAN
    full = lambda blk: nv_ref[blk] == GRAN_PER_BLOCK

    @pl.when((b == 0) | (be_ref[b] != be_ref[jnp.maximum(b - 1, 0)]))
    def _():
        w1b[...] = w1_ref[0, 0].astype(BF16)
        w3b[...] = w3_ref[0, 0].astype(BF16)
        w2b[...] = w2_ref[0, 0].astype(BF16)

    def start_gather(blk, sl):
        for i in range(GRAN_PER_BLOCK):
            pltpu.make_async_copy(xs_hbm.at[src_ref[blk * GRAN_PER_BLOCK + i]],
                                  xin[sl].at[pl.ds(i * GRAN, GRAN), :], sem_in.at[sl]).start()

    def wait_gather(sl):
        for i in range(GRAN_PER_BLOCK):
            pltpu.make_async_copy(xs_hbm.at[0], xin[sl].at[pl.ds(i * GRAN, GRAN), :], sem_in.at[sl]).wait()

    def granule_out(blk, sl, i):
        return pltpu.make_async_copy(yout[sl].at[pl.ds(i * GRAN, GRAN), :],
                                     ys_hbm.at[src_ref[blk * GRAN_PER_BLOCK + i]], sem_out.at[sl])

    def ffn(sl, lo, hi):
        x = xin[sl][lo:hi, :]
        hid = (_silu(_dot(x, w1b[...])) * _dot(x, w3b[...])).astype(BF16)
        yout[sl][lo:hi, :] = _dot(hid, w2b[...]).astype(BF16)

    def scatter(blk, sl, wait):
        nv = nv_ref[blk]
        for i in range(GRAN_PER_BLOCK):
            @pl.when(i < nv)
            def _():
                if wait:
                    granule_out(blk, sl, i).wait()
                else:
                    granule_out(blk, sl, i).start()

    def steady_step(sl):
        wait_gather(sl)

        @pl.when(full(b - 2))
        def _():
            for i in range(GRAN_PER_BLOCK):
                granule_out(b - 2, sl, i).wait()

        @pl.when(jnp.logical_not(full(b - 2)))
        def _():
            scatter(b - 2, sl, True)

        start_gather(b + 1, 1 - sl)
        half = block_rows // 2
        for part in range(2):
            ffn(sl, part * half, (part + 1) * half)
            for i in range(part * GRAN_PER_BLOCK // 2, (part + 1) * GRAN_PER_BLOCK // 2):
                granule_out(b, sl, i).start()

    def general_step(sl):
        @pl.when(b == 0)
        def _():
            start_gather(b, sl)

        @pl.when(b + 1 < nb)
        def _():
            start_gather(b + 1, 1 - sl)

        wait_gather(sl)

        @pl.when(b >= 2)
        def _():
            scatter(b - 2, sl, True)

        ffn(sl, 0, block_rows)
        scatter(b, sl, False)

        @pl.when(b == nb - 1)
        def _():
            scatter(b, sl, True)

            @pl.when(b >= 1)
            def _():
                scatter(b - 1, 1 - sl, True)

    steady = (b >= 2) & (b + 1 < nb) & full(b)
    for sl in range(2):
        @pl.when((b < nb) & (lax.rem(b, 2) == sl) & steady)
        def _():
            steady_step(sl)

        @pl.when((b < nb) & (lax.rem(b, 2) == sl) & jnp.logical_not(steady))
        def _():
            general_step(sl)


def _moe_ffn(xs, plan, w1, w3, w2, layer, nb_max):
    nrows, d = xs.shape
    f = w1.shape[-1]
    rows = GRAN_PER_BLOCK * GRAN
    wspec = lambda shape: pl.BlockSpec((1, 1) + shape, lambda b, nv, be, nb, src: (layer, be[b], 0, 0))
    ys = pl.pallas_call(
        _moe_ffn_kernel,
        out_shape=jax.ShapeDtypeStruct((nrows // GRAN, GRAN, d), BF16),
        grid_spec=pltpu.PrefetchScalarGridSpec(
            num_scalar_prefetch=4, grid=(nb_max,),
            in_specs=[pl.BlockSpec(memory_space=pl.ANY), wspec((d, f)), wspec((d, f)), wspec((f, d))],
            out_specs=pl.BlockSpec(memory_space=pl.ANY),
            scratch_shapes=[pltpu.VMEM((rows, d), BF16)] * 4
            + [pltpu.VMEM((d, f), BF16), pltpu.VMEM((d, f), BF16), pltpu.VMEM((f, d), BF16),
               pltpu.SemaphoreType.DMA((2,)), pltpu.SemaphoreType.DMA((2,))]),
        input_output_aliases={4: 0},
        compiler_params=pltpu.CompilerParams(dimension_semantics=("arbitrary",),
                                             vmem_limit_bytes=VMEM_LIMIT),
        name="moe_ffn",
    )(*plan, xs.reshape(nrows // GRAN, GRAN, d), w1, w3, w2)
    return ys.reshape(nrows, d)


def _final_kernel(xb_ref, ys_ref, posw_ref, mod_ref, nw_ref, o_ref, acc_ref):
    x = xb_ref[0] + mod_ref[0, 0][5:6] * _combine(ys_ref, posw_ref[0, 0], acc_ref)
    o_ref[0] = _rmsnorm(x, nw_ref[...])


def _final(xb, ys, posw, modx, nw, n_ctx_tiles):
    bsz, ntok, d = xb.shape
    nt_all = ntok // TILE
    nt = nt_all - n_ctx_tiles
    lat = lambda w: pl.BlockSpec((1, TILE, w), lambda b, j: (b, j + n_ctx_tiles, 0))
    return pl.pallas_call(
        _final_kernel,
        out_shape=jax.ShapeDtypeStruct((bsz, nt * TILE, d), F32),
        grid=(bsz, nt),
        scratch_shapes=[pltpu.VMEM((TILE, d), F32)],
        in_specs=[lat(d), pl.BlockSpec((SORT_ROWS, d), lambda b, j: (b * nt_all + j + n_ctx_tiles, 0)),
                  pl.BlockSpec((1, 1, 16, TILE), lambda b, j: (b, j + n_ctx_tiles, 0, 0)),
                  pl.BlockSpec((1, 1, 6, d), lambda b, j: (b, 1, 0, 0)),
                  pl.BlockSpec(nw.shape, lambda b, j: (0, 0))],
        out_specs=pl.BlockSpec((1, TILE, d), lambda b, j: (b, j, 0)),
        compiler_params=pltpu.CompilerParams(dimension_semantics=("parallel", "parallel"),
                                             vmem_limit_bytes=VMEM_LIMIT),
        name="final_norm",
    )(xb, ys, posw, modx, nw)


def kernel(x, c, ctx, c_ctx, norm1_w, norm2_w, w_ada, b_ada, w_in, hgrn_lb_logits, hgrn_gnorm_w, w_pool,
           pool_scale, w_out, w_router, router_bias, w1_experts, w3_experts, w2_experts, w1_shared,
           w3_shared, w2_shared, final_norm_w):
    bsz, seq_len, d = x.shape
    ctx_len = ctx.shape[1]
    depth = w_ada.shape[0]
    assert ctx_len == TILE and seq_len % TILE == 0 and seq_len % GRID_W == 0
    ntok = ctx_len + seq_len
    ntile = bsz * ntok // TILE
    nb_max = -(-ntile * MAX_GRAN_PER_TILE // GRAN_PER_BLOCK) + N_EXPERTS

    lf, lbm = _level_matrices()
    lower, upper = _tri_matrices()
    pa, pinv = _pool_matrices()
    consts = (jnp.asarray(np.stack([lower, upper]), BF16), jnp.asarray(np.maximum(lf, lbm)),
              jnp.asarray(pa, BF16), jnp.asarray(pinv))

    p_lb = jax.nn.softmax(hgrn_lb_logits.astype(F32), axis=0)
    lb_all = jnp.cumsum(p_lb, axis=0) - p_lb[0:1]

    rows = 16
    cvec = jnp.concatenate([c, c_ctx[None, :], jnp.zeros((rows - bsz - 1, d), F32)], axis=0)
    mod = _ada_table(cvec, w_ada, b_ada)
    mod_lat = mod[:, :bsz].reshape(depth, bsz, 6, d)
    mod_ctx = jnp.broadcast_to(mod[:, bsz].reshape(depth, 1, 6, d), (depth, bsz, 6, d))
    modx = jnp.stack([mod_ctx, mod_lat], axis=2)

    xs = (ctx, x)
    for l in range(depth):
        has_prev = l > 0
        outs = _mixer_in(has_prev, xs + ((modx[l - 1],) if has_prev else ()), modx[l], norm1_w[l][None, :],
                         w_in[l].astype(BF16), lb_all[l], consts, w_pool[l].astype(BF16),
                         pool_scale[l][None, :])
        oi, qe, ut, a, og, po = outs[:6]
        x_cur = (outs[6],) if has_prev else xs
        sf, sb = _scan_state(ut, a)
        wr_t = w_router[l].T
        wr_hi = wr_t.astype(BF16)
        xb, rows_sorted, posw, cnt = _mixer_out(
            x_cur, oi, qe, sf, sb, og, po, modx[l], hgrn_gnorm_w[l][None, :], w_out[l].astype(BF16),
            norm2_w[l][None, :], jnp.stack([wr_hi, (wr_t - wr_hi.astype(F32)).astype(BF16)]),
            router_bias[l][:, None], w1_shared[l].astype(BF16),
            w3_shared[l].astype(BF16), w2_shared[l].astype(BF16), consts[0])
        plan = _moe_plan(cnt[:, :, :, 0].reshape(ntile, N_EXPERTS).astype(jnp.int32), nb_max)
        ys = _moe_ffn(rows_sorted, plan, w1_experts, w3_experts, w2_experts, l, nb_max)
        xs = (xb, ys, posw)
    return _final(xs[0], xs[1], xs[2], modx[depth - 1], final_norm_w[None, :], ctx_len // TILE)
```
